```python
import math
import jax, jax.numpy as jnp
from jax import lax
import numpy as np

D_MODEL = 1024
BATCH = 32
SEQ = 2048
DEPTH = 4

N_META = 16
BLK = 128
WINDOW = 128
HEAD_DIM = 64
A_HEADS = 8
A_KV_HEADS = 2
B_HEADS = 8
C_HEADS = 8
C_Q_RANK = 256
C_KV_RANK = 128
C_NOPE = 64
C_ROPE = 32
C_V = 64
ROPE_THETA = 10000.0
N_BRANCH = 3
BRANCH_W = 512
EPS = 1e-6
NEG = -1e30

A_Q = A_HEADS * HEAD_DIM
A_KV = A_KV_HEADS * HEAD_DIM
B_QKV = B_HEADS * HEAD_DIM
SPLITS = (A_Q, A_KV, A_KV, B_QKV, B_QKV, B_QKV, B_HEADS,
          C_Q_RANK, C_KV_RANK, C_ROPE, N_BRANCH * BRANCH_W, N_BRANCH * D_MODEL)
D_IN = sum(SPLITS)
SPLIT_IDX = tuple(int(v) for v in np.cumsum(SPLITS)[:-1])

kernel_name = 'hybrid_swa_fox_mla_gated_trunk'


def rmsnorm(x, g):
    xf = x.astype(jnp.float32)
    y = xf * lax.rsqrt(jnp.mean(xf * xf, axis=-1, keepdims=True) + EPS)
    return (y * g.astype(jnp.float32)).astype(x.dtype)


def alibi_slopes(n):
    return 2.0 ** (-8.0 * (jnp.arange(n, dtype=jnp.float32) + 1.0) / n)


def rope(x, pos):
    half = x.shape[-1] // 2
    inv = ROPE_THETA ** (-jnp.arange(half, dtype=jnp.float32) / half)
    ang = pos[:, None] * inv[None, :]
    cos = jnp.cos(ang)[None, :, None, :]
    sin = jnp.sin(ang)[None, :, None, :]
    xf = x.astype(jnp.float32)
    x1, x2 = xf[..., :half], xf[..., half:]
    return jnp.concatenate([x1 * cos - x2 * sin, x2 * cos + x1 * sin], axis=-1).astype(x.dtype)


def swa_sink_attention(q, k, v, sinks, valid):
    b, L, hq, d = q.shape
    hkv = k.shape[2]
    grp = hq // hkv
    nb = L // BLK
    qr = q.reshape(b, nb, BLK, hkv, grp, d)

    def band(t):
        tb = t.reshape(b, nb, BLK, hkv, t.shape[-1])
        prev = jnp.concatenate([jnp.zeros_like(tb[:, :1]), tb[:, :-1]], axis=1)
        return jnp.concatenate([prev, tb], axis=2)

    kw, vw = band(k), band(v)
    vb = valid.reshape(nb, BLK)
    vprev = jnp.concatenate([jnp.zeros((1, BLK), dtype=bool), vb[:-1]], axis=0)
    validk = jnp.concatenate([vprev, vb], axis=1)
    dist = BLK + jnp.arange(BLK)[:, None] - jnp.arange(2 * BLK)[None, :]
    mask = ((dist >= 0) & (dist < WINDOW))[None] & validk[:, None, :]
    slopes = alibi_slopes(hq).reshape(hkv, grp)
    s = jnp.einsum('bnqhgd,bnshd->bnhgqs', qr, kw).astype(jnp.float32) * (d ** -0.5)
    s = s - slopes[:, :, None, None] * dist.astype(jnp.float32)
    s = jnp.where(mask[None, :, None, None], s, NEG)
    sink = jnp.broadcast_to(sinks.astype(jnp.float32).reshape(hkv, grp)[None, None, :, :, None, None],
                            s.shape[:-1] + (1,))
    p = jax.nn.softmax(jnp.concatenate([s, sink], axis=-1), axis=-1)[..., :-1].astype(v.dtype)
    o = jnp.einsum('bnhgqs,bnshd->bnqhgd', p, vw)
    return o.reshape(b, L, hq * d)


def dense_causal_attention(q, k, v, valid, scale, log_cum=None):
    b, L, h, _ = q.shape
    nb = L // BLK
    idx = jnp.arange(L)
    fh = None if log_cum is None else jnp.transpose(log_cum, (0, 2, 1))
    outs = []
    for i in range(nb):
        lo, hi = i * BLK, (i + 1) * BLK
        s = jnp.einsum('bqhd,bkhd->bhqk', q[:, lo:hi], k[:, :hi]).astype(jnp.float32) * scale
        if fh is not None:
            s = s + (fh[:, :, lo:hi, None] - fh[:, :, None, :hi])
        mask = (idx[None, :hi] <= idx[lo:hi, None]) & valid[None, :hi]
        s = jnp.where(mask[None, None], s, NEG)
        p = jax.nn.softmax(s, axis=-1).astype(v.dtype)
        outs.append(jnp.einsum('bhqk,bkhd->bqhd', p, v[:, :hi]))
    return jnp.concatenate(outs, axis=1).reshape(b, L, -1)


def hybrid_layer(x, pos, valid, norm_g, w_in, b_f, sinks, q_norm_g, kv_norm_g, w_uq, w_ukv, w_br, w_out):
    b, L, _ = x.shape
    h = rmsnorm(x, norm_g)
    (aq, ak, av, bq, bk, bv, bfl, cq, ckv, ckr, z, g) = jnp.split(h @ w_in, SPLIT_IDX, axis=-1)

    oa = swa_sink_attention(aq.reshape(b, L, A_HEADS, HEAD_DIM),
                            ak.reshape(b, L, A_KV_HEADS, HEAD_DIM),
                            av.reshape(b, L, A_KV_HEADS, HEAD_DIM), sinks, valid)

    logf = jax.nn.log_sigmoid((bfl + b_f).astype(jnp.float32))
    logf = jnp.where(valid[None, :, None], logf, 0.0)
    fcum = jnp.cumsum(logf, axis=1)
    ob = dense_causal_attention(bq.reshape(b, L, B_HEADS, HEAD_DIM),
                                bk.reshape(b, L, B_HEADS, HEAD_DIM),
                                bv.reshape(b, L, B_HEADS, HEAD_DIM),
                                valid, HEAD_DIM ** -0.5, fcum)

    qc = (rmsnorm(cq, q_norm_g) @ w_uq).reshape(b, L, C_HEADS, C_NOPE + C_ROPE)
    kvc = (rmsnorm(ckv, kv_norm_g) @ w_ukv).reshape(b, L, C_HEADS, C_NOPE + C_V)
    k_rope = rope(ckr[:, :, None, :], pos)
    qc = jnp.concatenate([qc[..., :C_NOPE], rope(qc[..., C_NOPE:], pos)], axis=-1)
    kc = jnp.concatenate([kvc[..., :C_NOPE],
                          jnp.broadcast_to(k_rope, (b, L, C_HEADS, C_ROPE))], axis=-1)
    oc = dense_causal_attention(qc, kc, kvc[..., C_NOPE:], valid, (C_NOPE + C_ROPE) ** -0.5)

    zs = jnp.split(z, N_BRANCH, axis=-1)
    gs = jnp.split(g, N_BRANCH, axis=-1)
    branches = (oa, ob, oc)
    y = jax.nn.sigmoid(gs[0]) * ((branches[0] * jax.nn.silu(zs[0])) @ w_br[0])
    for i in range(1, N_BRANCH):
        y = y + jax.nn.sigmoid(gs[i]) * ((branches[i] * jax.nn.silu(zs[i])) @ w_br[i])
    return x + y @ w_out


def setup_inputs(seed: int = 0) -> dict:
    key = jax.random.key(seed)
    ks = jax.random.split(key, 13)
    f32 = jnp.float32
    x = jax.random.normal(ks[0], (BATCH, SEQ, D_MODEL), f32)
    meta_tokens = jax.random.normal(ks[1], (N_META, D_MODEL), f32)
    norm_g = 1.0 + 0.05 * jax.random.normal(ks[2], (DEPTH, D_MODEL), f32)
    w_in = jax.random.normal(ks[3], (DEPTH, D_MODEL, D_IN), f32) * D_MODEL ** -0.5
    b_f = 1.0 + 3.0 * jax.random.uniform(ks[4], (DEPTH, B_HEADS), f32)
    sinks = 0.5 * jax.random.normal(ks[5], (DEPTH, A_HEADS), f32)
    q_norm_g = 1.0 + 0.05 * jax.random.normal(ks[6], (DEPTH, C_Q_RANK), f32)
    kv_norm_g = 1.0 + 0.05 * jax.random.normal(ks[7], (DEPTH, C_KV_RANK), f32)
    w_uq = jax.random.normal(ks[8], (DEPTH, C_Q_RANK, C_HEADS * (C_NOPE + C_ROPE)), f32) * C_Q_RANK ** -0.5
    w_ukv = jax.random.normal(ks[9], (DEPTH, C_KV_RANK, C_HEADS * (C_NOPE + C_V)), f32) * C_KV_RANK ** -0.5
    w_br = jax.random.normal(ks[10], (DEPTH, N_BRANCH, BRANCH_W, D_MODEL), f32) * BRANCH_W ** -0.5
    w_out = jax.random.normal(ks[11], (DEPTH, D_MODEL, D_MODEL), f32) * D_MODEL ** -0.5
    final_norm_g = 1.0 + 0.05 * jax.random.normal(ks[12], (D_MODEL,), f32)
    return {'x': x, 'meta_tokens': meta_tokens, 'norm_g': norm_g, 'w_in': w_in, 'b_f': b_f,
            'sinks': sinks, 'q_norm_g': q_norm_g, 'kv_norm_g': kv_norm_g, 'w_uq': w_uq,
            'w_ukv': w_ukv, 'w_br': w_br, 'w_out': w_out, 'final_norm_g': final_norm_g}


def reference(x, meta_tokens, norm_g, w_in, b_f, sinks, q_norm_g, kv_norm_g, w_uq, w_ukv, w_br, w_out, final_norm_g):
    b = x.shape[0]
    pad = BLK - N_META
    h = jnp.concatenate([jnp.zeros((b, pad, D_MODEL), x.dtype),
                         jnp.broadcast_to(meta_tokens[None].astype(x.dtype), (b, N_META, D_MODEL)),
                         x], axis=1)
    L = h.shape[1]
    idx = jnp.arange(L)
    pos = (idx - pad).astype(jnp.float32)
    valid = idx >= pad
    for l in range(DEPTH):
        h = hybrid_layer(h, pos, valid, norm_g[l], w_in[l], b_f[l], sinks[l], q_norm_g[l],
                         kv_norm_g[l], w_uq[l], w_ukv[l], w_br[l], w_out[l])
    return rmsnorm(h[:, BLK:], final_norm_g)
```

```python
import functools
import math

import jax
import jax.numpy as jnp
import numpy as np
from jax import lax
from jax.experimental import pallas as pl
from jax.experimental.pallas import tpu as pltpu

D_MODEL = 1024
N_META = 16
BLK = 128
PAD = BLK - N_META
HEAD_DIM = 64
N_HEADS = 8
A_KV_HEADS = 2
C_Q_RANK = 256
C_KV_RANK = 128
C_NOPE = 64
C_ROPE = 32
ROPE_HALF = C_ROPE // 2
ROPE_THETA = 10000.0
N_BRANCH = 3
BRANCH_W = 512
EPS = 1e-6
NEG = -1e30
LOG2E = math.log2(math.e)

LANES = 128
Q_TILE = 256
PRE_TM = 512
POST_TM = 512
VMEM_LIMIT_BYTES = 56 * 1024 * 1024

SCALE_AB = HEAD_DIM ** -0.5 * LOG2E
SCALE_C = (C_NOPE + C_ROPE) ** -0.5 * LOG2E

COL_AQ = 0
COL_AK = 512
COL_AV = 640
COL_BQ = 768
COL_BK = 1280
COL_BV = 1792
COL_CQ = 2304
COL_CKV = 2560
COL_MISC = 2688
N_PRE = 2816
MISC_ROPE = 64
AUG_ONE = 24

BF16 = jnp.bfloat16
F32 = jnp.float32


def _dot(a, b):
    return jnp.dot(a, b, preferred_element_type=F32)


def _dot_nt(a, b):
    return lax.dot_general(a, b, (((1,), (1,)), ((), ())), preferred_element_type=F32)


def _rms(x, g):
    return x * lax.rsqrt(jnp.mean(x * x, axis=-1, keepdims=True) + EPS) * g


def _split3(x):
    hi = x.astype(BF16).astype(F32)
    r = x - hi
    mid = r.astype(BF16).astype(F32)
    lo = (r - mid).astype(BF16).astype(F32)
    return hi, mid, lo


def _pre_kernel(x_ref, g_ref, w1_ref, wuq_ref, wuk_ref, wuv_ref, eq_ref, ek_ref, qng_ref, kvng_ref,
                bf_ref, tab_ref, qk_ref, vt_ref, qa_ref, ka_ref, vta_ref, carry_ref, *, tm, seq):
    i = pl.program_id(0)
    hb = _rms(x_ref[...], g_ref[...]).astype(BF16)
    lane = lax.broadcasted_iota(jnp.int32, (tm, LANES), 1)
    row = lax.broadcasted_iota(jnp.int32, (tm, LANES), 0)
    lo_half = lane < HEAD_DIM

    ra = _dot(hb, w1_ref[:, COL_AQ:COL_BQ])
    for j in range(N_HEADS // 2):
        pq = ra[:, LANES * j:LANES * (j + 1)] * SCALE_AB
        qa_ref[j] = jnp.where(lo_half, pq, 0.0).astype(BF16)
        qa_ref[j + N_HEADS // 2] = jnp.where(lo_half, 0.0, pq).astype(BF16)
    ka_ref[...] = ra[:, COL_AK:COL_AV].astype(BF16)
    vta_ref[...] = ra[:, COL_AV:COL_BQ].T.astype(BF16)

    misc = _dot(hb, w1_ref[:, COL_MISC:N_PRE])
    p0 = lax.rem(i * tm, seq)
    pib = p0 + row
    pib = jnp.where(pib >= seq, pib - seq, pib)
    xg = misc + bf_ref[...]
    logf = (jnp.minimum(xg, 0.0) - jnp.log1p(jnp.exp(-jnp.abs(xg)))) * LOG2E
    logf = jnp.where((pib >= PAD) & (lane < N_HEADS), logf, 0.0)

    def pieces(v):
        rep = v + pltpu.roll(v, N_HEADS, 1) + pltpu.roll(v, 2 * N_HEADS, 1)
        hi, mid, lo = _split3(rep)
        return jnp.where(lane < N_HEADS, hi, jnp.where(lane < 2 * N_HEADS, mid, lo))

    def gather_pieces(v):
        return v + pltpu.roll(v, LANES - N_HEADS, 1) + pltpu.roll(v, LANES - 2 * N_HEADS, 1)

    bnd = seq - p0
    tr = lax.broadcasted_iota(jnp.int32, (tm, tm), 0)
    tc = lax.broadcasted_iota(jnp.int32, (tm, tm), 1)
    tri = jnp.where((tc <= tr) & ((tc >= bnd) | (tr < bnd)), 1.0, 0.0).astype(BF16)
    cum = gather_pieces(_dot(tri, pieces(logf).astype(BF16)))

    @pl.when(i == 0)
    def _():
        carry_ref[...] = jnp.zeros_like(carry_ref)

    carry_in = jnp.where(p0 == 0, 0.0, carry_ref[0:1, :])
    cum = jnp.where(lane < N_HEADS, cum + jnp.where(row >= bnd, 0.0, carry_in), 0.0)
    carry_ref[...] = jnp.broadcast_to(cum[tm - 1:tm, :], carry_ref.shape)
    pc = jnp.where(lane == AUG_ONE, 1.0, pieces(cum)).astype(BF16)
    augq = _dot(pc, eq_ref[...])
    augk = _dot(pc, ek_ref[...])

    rb = _dot(hb, w1_ref[:, COL_BQ:COL_CQ])
    for h in range(N_HEADS):
        j, t = divmod(h, 2)
        own = lo_half if t == 0 else jnp.logical_not(lo_half)
        pq = rb[:, LANES * j:LANES * (j + 1)] * SCALE_AB
        pk = rb[:, BRANCH_W + LANES * j:BRANCH_W + LANES * (j + 1)]
        qk_ref[h] = jnp.where(own, pq, augq[:, LANES * h:LANES * (h + 1)]).astype(BF16)
        qk_ref[2 * N_HEADS + h] = jnp.where(own, pk, augk[:, LANES * h:LANES * (h + 1)]).astype(BF16)
    vt_ref[0:BRANCH_W, :] = rb[:, 2 * BRANCH_W:3 * BRANCH_W].T.astype(BF16)

    rc = _dot(hb, w1_ref[:, COL_CQ:COL_MISC])
    cqn = _rms(rc[:, :C_Q_RANK], qng_ref[...]).astype(BF16)
    ckvn = _rms(rc[:, C_Q_RANK:], kvng_ref[...]).astype(BF16)

    def rope(v, c, s1, s2):
        return v * c + pltpu.roll(v, LANES - ROPE_HALF, 1) * s1 + pltpu.roll(v, ROPE_HALF, 1) * s2

    qc = _dot(cqn, wuq_ref[...])
    for h in range(N_HEADS):
        blk = qc[:, LANES * h:LANES * (h + 1)]
        qk_ref[N_HEADS + h] = rope(blk, tab_ref[0], tab_ref[1], tab_ref[2]).astype(BF16)
    kc = _dot(ckvn, wuk_ref[...])
    kr = rope(misc, tab_ref[3], tab_ref[4], tab_ref[5])
    for h in range(N_HEADS):
        qk_ref[3 * N_HEADS + h] = (kc[:, LANES * h:LANES * (h + 1)] + kr).astype(BF16)
    vt_ref[BRANCH_W:2 * BRANCH_W, :] = _dot(ckvn, wuv_ref[...]).T.astype(BF16)


def _const_spec(shape):
    nd = len(shape)
    return pl.BlockSpec(shape, lambda *_: (0,) * nd)


def _pre_call(x, lw, consts, *, seq):
    m = x.shape[0]
    tm = PRE_TM
    n_tiles = m // tm
    period = seq // LANES
    grid_spec = pl.GridSpec(
        grid=(n_tiles,),
        in_specs=[
            pl.BlockSpec((tm, D_MODEL), lambda i: (i, 0)),
            _const_spec((1, D_MODEL)),
            _const_spec((D_MODEL, N_PRE)),
            _const_spec((C_Q_RANK, N_HEADS * LANES)),
            _const_spec((C_KV_RANK, N_HEADS * LANES)),
            _const_spec((C_KV_RANK, BRANCH_W)),
            _const_spec((LANES, N_HEADS * LANES)),
            _const_spec((LANES, N_HEADS * LANES)),
            _const_spec((1, C_Q_RANK)),
            _const_spec((1, C_KV_RANK)),
            _const_spec((1, LANES)),
            pl.BlockSpec((6, tm, LANES), lambda i: (0, i % period, 0)),
        ],
        out_specs=[
            pl.BlockSpec((4 * N_HEADS, tm, LANES), lambda i: (0, i, 0)),
            pl.BlockSpec((2 * BRANCH_W, tm), lambda i: (0, i)),
            pl.BlockSpec((N_HEADS, tm, LANES), lambda i: (0, i, 0)),
            pl.BlockSpec((tm, LANES), lambda i: (i, 0)),
            pl.BlockSpec((LANES, tm), lambda i: (0, i)),
        ],
        scratch_shapes=[pltpu.VMEM((8, LANES), F32)],
    )
    out_shape = [
        jax.ShapeDtypeStruct((4 * N_HEADS, m, LANES), BF16),
        jax.ShapeDtypeStruct((2 * BRANCH_W, m), BF16),
        jax.ShapeDtypeStruct((N_HEADS, m, LANES), BF16),
        jax.ShapeDtypeStruct((m, LANES), BF16),
        jax.ShapeDtypeStruct((LANES, m), BF16),
    ]
    return pl.pallas_call(
        functools.partial(_pre_kernel, tm=tm, seq=seq),
        grid_spec=grid_spec,
        out_shape=out_shape,
        compiler_params=pltpu.CompilerParams(
            dimension_semantics=("arbitrary",), vmem_limit_bytes=VMEM_LIMIT_BYTES),
        name="pre",
    )(x, lw["norm_g"], lw["w1"], lw["wuq"], lw["wuk"], lw["wuv"], consts["eq"], consts["ek"],
      lw["q_norm_g"], lw["kv_norm_g"], lw["b_f"], consts["rope_tab"])


def _swa_kernel(q_ref, k_ref, vt_ref, bias_ref, sink_ref, o_ref, ot_scr, *, seq):
    nb = seq // BLK
    half = N_HEADS // 2 * BLK
    sink = sink_ref[...]
    for n in range(nb):
        q = q_ref[:, n * BLK:(n + 1) * BLK, :].reshape(N_HEADS * BLK, LANES)
        k0 = max(n - 1, 0) * BLK
        k1 = (n + 1) * BLK
        if n == 0:
            bias = bias_ref[2, BLK:, :]
        else:
            bias = bias_ref[0 if n >= 2 else 1]
        s = _dot_nt(k_ref[k0:k1, :], q) + bias
        m = jnp.maximum(jnp.max(s, axis=0, keepdims=True), sink)
        p = jnp.exp2(s - m)
        inv = 1.0 / (jnp.sum(p, axis=0, keepdims=True) + jnp.exp2(sink - m))
        pb = p.astype(BF16)
        for kv in range(A_KV_HEADS):
            pv = _dot(vt_ref[kv * HEAD_DIM:(kv + 1) * HEAD_DIM, k0:k1], pb[:, kv * half:(kv + 1) * half])
            pv = pv * inv[:, kv * half:(kv + 1) * half]
            for g in range(N_HEADS // A_KV_HEADS):
                h = kv * (N_HEADS // A_KV_HEADS) + g
                ot_scr[h * HEAD_DIM:(h + 1) * HEAD_DIM, n * BLK:(n + 1) * BLK] = pv[:, g * BLK:(g + 1) * BLK]
    for c in range(nb):
        o_ref[c * BLK:(c + 1) * BLK, :] = ot_scr[:, c * BLK:(c + 1) * BLK].T.astype(o_ref.dtype)


def _swa_call(qa, ka, vta, bias, sink, *, batch, seq):
    m = ka.shape[0]
    grid_spec = pl.GridSpec(
        grid=(batch,),
        in_specs=[
            pl.BlockSpec((N_HEADS, seq, LANES), lambda b: (0, b, 0)),
            pl.BlockSpec((seq, LANES), lambda b: (b, 0)),
            pl.BlockSpec((LANES, seq), lambda b: (0, b)),
            _const_spec((3, 2 * BLK, N_HEADS * BLK)),
            _const_spec((1, N_HEADS * BLK)),
        ],
        out_specs=pl.BlockSpec((seq, BRANCH_W), lambda b: (b, 0)),
        scratch_shapes=[pltpu.VMEM((BRANCH_W, seq), F32)],
    )
    return pl.pallas_call(
        functools.partial(_swa_kernel, seq=seq),
        grid_spec=grid_spec,
        out_shape=jax.ShapeDtypeStruct((m, BRANCH_W), BF16),
        compiler_params=pltpu.CompilerParams(
            dimension_semantics=("parallel",), vmem_limit_bytes=VMEM_LIMIT_BYTES),
        name="swa",
    )(qa, ka, vta, bias, sink)


def _dense_kernel(q_ref, k_ref, vt_ref, dbias_ref, mbias_ref, mqbias_ref, o_ref, ot_scr, *, seq):
    n_q = (seq - BLK) // Q_TILE

    def step(kt, vt, q, bias, state):
        s = _dot_nt(kt, q)
        if bias is not None:
            s = s + bias
        mc = jnp.max(s, axis=0, keepdims=True)
        if state is None:
            p = jnp.exp2(s - mc)
            return mc, jnp.sum(p, axis=0, keepdims=True), _dot(vt, p.astype(BF16))
        m, l, acc = state
        m_new = jnp.maximum(m, mc)
        alpha = jnp.exp2(m - m_new)
        p = jnp.exp2(s - m_new)
        return (m_new, alpha * l + jnp.sum(p, axis=0, keepdims=True),
                alpha * acc + _dot(vt, p.astype(BF16)))

    def head_body(h, carry):
        r0 = pl.multiple_of(h * HEAD_DIM, HEAD_DIM)
        rows = pl.ds(r0, HEAD_DIM)
        k_meta = k_ref[h, 0:BLK, :]
        v_meta = vt_ref[rows, 0:BLK]
        _, l, acc = step(k_meta, v_meta, q_ref[h, 0:BLK, :], mqbias_ref[...], None)
        ot_scr[rows, 0:BLK] = acc / l
        for i in range(n_q):
            q0 = BLK + Q_TILE * i
            q = q_ref[h, q0:q0 + Q_TILE, :]
            state = step(k_meta, v_meta, q, mbias_ref[...], None)
            for j in range(i + 1):
                c0 = BLK + Q_TILE * j
                state = step(k_ref[h, c0:c0 + Q_TILE, :], vt_ref[rows, c0:c0 + Q_TILE], q,
                             dbias_ref[...] if j == i else None, state)
            _, l, acc = state
            ot_scr[rows, q0:q0 + Q_TILE] = acc / l
        return carry

    lax.fori_loop(0, N_HEADS, head_body, 0)
    for c in range(seq // BLK):
        o_ref[0, c * BLK:(c + 1) * BLK, :] = ot_scr[:, c * BLK:(c + 1) * BLK].T.astype(o_ref.dtype)


def _dense_call(qk, vt, consts, *, batch, seq):
    m = qk.shape[1]
    grid_spec = pl.GridSpec(
        grid=(batch, 2),
        in_specs=[
            pl.BlockSpec((N_HEADS, seq, LANES), lambda b, x: (x, b, 0)),
            pl.BlockSpec((N_HEADS, seq, LANES), lambda b, x: (2 + x, b, 0)),
            pl.BlockSpec((BRANCH_W, seq), lambda b, x: (x, b)),
            _const_spec((Q_TILE, Q_TILE)),
            _const_spec((BLK, Q_TILE)),
            _const_spec((BLK, BLK)),
        ],
        out_specs=pl.BlockSpec((1, seq, BRANCH_W), lambda b, x: (x, b, 0)),
        scratch_shapes=[pltpu.VMEM((BRANCH_W, seq), F32)],
    )
    return pl.pallas_call(
        functools.partial(_dense_kernel, seq=seq),
        grid_spec=grid_spec,
        out_shape=jax.ShapeDtypeStruct((2, m, BRANCH_W), BF16),
        compiler_params=pltpu.CompilerParams(
            dimension_semantics=("parallel", "parallel"), vmem_limit_bytes=VMEM_LIMIT_BYTES),
        name="dense",
    )(qk, qk, vt, consts["dbias"], consts["mbias"], consts["mqbias"])


def _post_kernel(x_ref, oa_ref, obc_ref, g_ref, wz_ref, wg_ref, wbr_ref, wout_ref, fg_ref, out_ref, *, final):
    x = x_ref[...]
    hb = _rms(x, g_ref[...]).astype(BF16)
    y = None
    for i in range(N_BRANCH):
        o = (oa_ref[...] if i == 0 else obc_ref[i - 1]).astype(F32)
        z = _dot(hb, wz_ref[:, BRANCH_W * i:BRANCH_W * (i + 1)])
        u = (o * (z * jax.nn.sigmoid(z))).astype(BF16)
        t = _dot(u, wbr_ref[i])
        gate = jax.nn.sigmoid(_dot(hb, wg_ref[:, D_MODEL * i:D_MODEL * (i + 1)]))
        y = gate * t if y is None else y + gate * t
    out = x + _dot(y.astype(BF16), wout_ref[...])
    if final:
        out = _rms(out, fg_ref[...])
    out_ref[...] = out


def _post_call(x, oa, obc, lw, final_g, *, final):
    m = x.shape[0]
    tm = POST_TM
    grid_spec = pl.GridSpec(
        grid=(m // tm,),
        in_specs=[
            pl.BlockSpec((tm, D_MODEL), lambda i: (i, 0)),
            pl.BlockSpec((tm, BRANCH_W), lambda i: (i, 0)),
            pl.BlockSpec((2, tm, BRANCH_W), lambda i: (0, i, 0)),
            _const_spec((1, D_MODEL)),
            _const_spec((D_MODEL, N_BRANCH * BRANCH_W)),
            _const_spec((D_MODEL, N_BRANCH * D_MODEL)),
            _const_spec((N_BRANCH, BRANCH_W, D_MODEL)),
            _const_spec((D_MODEL, D_MODEL)),
            _const_spec((1, D_MODEL)),
        ],
        out_specs=pl.BlockSpec((tm, D_MODEL), lambda i: (i, 0)),
    )
    return pl.pallas_call(
        functools.partial(_post_kernel, final=final),
        grid_spec=grid_spec,
        out_shape=jax.ShapeDtypeStruct((m, D_MODEL), F32),
        input_output_aliases={0: 0},
        compiler_params=pltpu.CompilerParams(
            dimension_semantics=("parallel",), vmem_limit_bytes=VMEM_LIMIT_BYTES),
        name="post",
    )(x, oa, obc, lw["norm_g"], lw["wz"], lw["wg"], lw["w_br"], lw["w_out"], final_g)


def _rope_tables(seq, tm):
    pos = (jnp.arange(seq) - PAD).astype(F32)
    inv = ROPE_THETA ** (-jnp.arange(ROPE_HALF, dtype=F32) / ROPE_HALF)
    ang = pos[:, None] * inv[None, :]
    cos, sin = jnp.cos(ang), jnp.sin(ang)
    zero = jnp.zeros_like(cos)

    def lanes(nope, a, b):
        return jnp.concatenate([jnp.full((seq, C_NOPE), nope, F32), a, b,
                                jnp.zeros((seq, LANES - C_NOPE - C_ROPE), F32)], axis=1)

    tabs = jnp.stack([lanes(1.0, cos, cos) * SCALE_C, lanes(0.0, -sin, zero) * SCALE_C,
                      lanes(0.0, zero, sin) * SCALE_C,
                      lanes(0.0, cos, cos), lanes(0.0, -sin, zero), lanes(0.0, zero, sin)])
    return jnp.tile(tabs, (1, tm // LANES, 1))


def _aug_matrices():
    eq = np.zeros((LANES, N_HEADS * LANES), np.float32)
    ek = np.zeros((LANES, N_HEADS * LANES), np.float32)
    for h in range(N_HEADS):
        a = LANES * h + (HEAD_DIM if h % 2 == 0 else 0)
        for p in range(3):
            eq[p * N_HEADS + h, a + p] = 1.0
            eq[AUG_ONE, a + 3 + p] = 1.0
            ek[AUG_ONE, a + p] = 1.0
            ek[p * N_HEADS + h, a + 3 + p] = -1.0
    return jnp.asarray(eq, BF16), jnp.asarray(ek, BF16)


def _dense_biases():
    kk = np.arange(Q_TILE)[:, None]
    qq = np.arange(Q_TILE)[None, :]
    dbias = np.where(kk <= qq, 0.0, NEG).astype(np.float32)
    mbias = np.where(np.arange(BLK)[:, None] >= PAD, 0.0, NEG) * np.ones((1, Q_TILE))
    km = np.arange(BLK)[:, None]
    qm = np.arange(BLK)[None, :]
    mqbias = np.where((km >= PAD) & (km <= qm), 0.0, NEG)
    return jnp.asarray(dbias, F32), jnp.asarray(mbias, F32), jnp.asarray(mqbias, F32)


def _swa_biases():
    slopes = 2.0 ** (-8.0 * (jnp.arange(N_HEADS, dtype=F32) + 1.0) / N_HEADS)
    s = np.arange(2 * BLK)[:, None]
    t = np.arange(BLK)[None, :]
    dist = BLK + t - s
    window = (dist >= 0) & (dist < BLK)
    tabs = []
    for mask in (window, window & (s >= PAD), window & (s >= BLK + PAD)):
        per_head = [jnp.where(mask, -slopes[h] * dist.astype(np.float32) * LOG2E, NEG) for h in range(N_HEADS)]
        tabs.append(jnp.concatenate(per_head, axis=1))
    return jnp.stack(tabs).astype(F32)


def _layer_weights(l, norm_g, w_in, b_f, sinks, q_norm_g, kv_norm_g, w_uq, w_ukv, w_br, w_out):
    w = w_in[l]
    splits = np.cumsum([0, 512, 128, 128, 512, 512, 512, 8, 256, 128, 32, 1536, 3072])
    aq, ak, av, bq, bk, bv, bfl, cq, ckv, ckr, z, g = [w[:, splits[i]:splits[i + 1]] for i in range(12)]
    aq = aq.reshape(D_MODEL, A_KV_HEADS, N_HEADS // A_KV_HEADS, HEAD_DIM)
    aq = jnp.transpose(aq, (0, 2, 1, 3)).reshape(D_MODEL, N_HEADS * HEAD_DIM)
    misc = jnp.concatenate([bfl, jnp.zeros((D_MODEL, MISC_ROPE - N_HEADS), F32), ckr,
                            jnp.zeros((D_MODEL, LANES - MISC_ROPE - C_ROPE), F32)], axis=1)
    w1 = jnp.concatenate([aq, ak, av, bq, bk, bv, cq, ckv, misc], axis=1).astype(BF16)
    wuq = jnp.pad(w_uq[l].reshape(C_Q_RANK, N_HEADS, C_NOPE + C_ROPE),
                  ((0, 0), (0, 0), (0, LANES - C_NOPE - C_ROPE))).reshape(C_Q_RANK, N_HEADS * LANES)
    wukv = w_ukv[l].reshape(C_KV_RANK, N_HEADS, 2 * HEAD_DIM)
    wuk = jnp.pad(wukv[:, :, :C_NOPE], ((0, 0), (0, 0), (0, LANES - C_NOPE))).reshape(C_KV_RANK, N_HEADS * LANES)
    wuv = wukv[:, :, C_NOPE:].reshape(C_KV_RANK, BRANCH_W)
    bf_row = jnp.pad(b_f[l], (0, LANES - N_HEADS)).reshape(1, LANES)
    sink_row = jnp.repeat(sinks[l] * LOG2E, BLK).reshape(1, N_HEADS * BLK)
    return {
        "norm_g": norm_g[l].reshape(1, D_MODEL), "w1": w1, "wuq": wuq.astype(BF16), "wuk": wuk.astype(BF16),
        "wuv": wuv.astype(BF16), "q_norm_g": q_norm_g[l].reshape(1, C_Q_RANK),
        "kv_norm_g": kv_norm_g[l].reshape(1, C_KV_RANK), "b_f": bf_row, "sink": sink_row,
        "wz": z.astype(BF16), "wg": g.astype(BF16), "w_br": w_br[l].astype(BF16), "w_out": w_out[l].astype(BF16),
    }


def kernel(x, meta_tokens, norm_g, w_in, b_f, sinks, q_norm_g, kv_norm_g, w_uq, w_ukv, w_br, w_out, final_norm_g):
    batch, n_real, _ = x.shape
    seq = BLK + n_real
    depth = w_in.shape[0]
    assert (seq - BLK) % Q_TILE == 0 and (batch * seq) % PRE_TM == 0 and (batch * seq) % POST_TM == 0
    h = jnp.concatenate([jnp.zeros((batch, PAD, D_MODEL), x.dtype),
                         jnp.broadcast_to(meta_tokens[None].astype(x.dtype), (batch, N_META, D_MODEL)),
                         x], axis=1).reshape(batch * seq, D_MODEL)
    eq, ek = _aug_matrices()
    dbias, mbias, mqbias = _dense_biases()
    consts = {"eq": eq, "ek": ek, "rope_tab": _rope_tables(seq, PRE_TM),
              "dbias": dbias, "mbias": mbias, "mqbias": mqbias}
    swa_bias = _swa_biases()
    final_g = final_norm_g.reshape(1, D_MODEL)
    for l in range(depth):
        lw = _layer_weights(l, norm_g, w_in, b_f, sinks, q_norm_g, kv_norm_g, w_uq, w_ukv, w_br, w_out)
        qk, vt, qa, ka, vta = _pre_call(h, lw, consts, seq=seq)
        oa = _swa_call(qa, ka, vta, swa_bias, lw["sink"], batch=batch, seq=seq)
        obc = _dense_call(qk, vt, consts, batch=batch, seq=seq)
        h = _post_call(h, oa, obc, lw, final_g, final=(l == depth - 1))
    return h.reshape(batch, seq, D_MODEL)[:, BLK:]
```

```python
import functools
import math

import jax
import jax.numpy as jnp
import numpy as np
from jax import lax
from jax.experimental import pallas as pl
from jax.experimental.pallas import tpu as pltpu

D_MODEL = 1024
N_META = 16
BLK = 128
PAD = BLK - N_META
HEAD_DIM = 64
N_HEADS = 8
A_KV_HEADS = 2
C_Q_RANK = 256
C_KV_RANK = 128
C_NOPE = 64
C_ROPE = 32
ROPE_HALF = C_ROPE // 2
ROPE_THETA = 10000.0
N_BRANCH = 3
BRANCH_W = 512
EPS = 1e-6
NEG = -1e30
LOG2E = math.log2(math.e)

LANES = 128
Q_TILE = 512
K_TILE = 512
LOOKAHEAD = 1
HEADS_PER_ITER = 2
SUM_ROWS = 16
PRE_TM = 512
POST_TM = 1024
VMEM_LIMIT_BYTES = 56 * 1024 * 1024

SCALE_AB = HEAD_DIM ** -0.5 * LOG2E
SCALE_C = (C_NOPE + C_ROPE) ** -0.5 * LOG2E

COL_AQ = 0
COL_AK = 512
COL_AV = 640
COL_BQ = 768
COL_BK = 1280
COL_BV = 1792
COL_CQ = 2304
COL_CKV = 2560
COL_MISC = 2688
N_PRE = 2816
MISC_ROPE = 64
AUG_ONE = 24

BF16 = jnp.bfloat16
F32 = jnp.float32


def _dot(a, b):
    return jnp.dot(a, b, preferred_element_type=F32)


def _dot_nt(a, b):
    return lax.dot_general(a, b, (((1,), (1,)), ((), ())), preferred_element_type=F32)


def _rms(x, g):
    return x * lax.rsqrt(jnp.mean(x * x, axis=-1, keepdims=True) + EPS) * g


def _split3(x):
    hi = x.astype(BF16).astype(F32)
    r = x - hi
    mid = r.astype(BF16).astype(F32)
    lo = (r - mid).astype(BF16).astype(F32)
    return hi, mid, lo


def _pre_kernel(x_ref, g_ref, w1_ref, wuq_ref, wuk_ref, wuv_ref, eq_ref, ek_ref, qng_ref, kvng_ref,
                bf_ref, tab_ref, qk_ref, vt_ref, qa_ref, ka_ref, vta_ref, carry_ref, *, tm, seq):
    i = pl.program_id(0)
    hb = _rms(x_ref[...], g_ref[...]).astype(BF16)
    lane = lax.broadcasted_iota(jnp.int32, (tm, LANES), 1)
    row = lax.broadcasted_iota(jnp.int32, (tm, LANES), 0)
    lo_half = lane < HEAD_DIM

    ra = _dot(hb, w1_ref[:, COL_AQ:COL_BQ])
    for j in range(N_HEADS // 2):
        pq = ra[:, LANES * j:LANES * (j + 1)] * SCALE_AB
        qa_ref[j] = jnp.where(lo_half, pq, 0.0).astype(BF16)
        qa_ref[j + N_HEADS // 2] = jnp.where(lo_half, 0.0, pq).astype(BF16)
    ka_ref[...] = ra[:, COL_AK:COL_AV].astype(BF16)
    vta_ref[...] = ra[:, COL_AV:COL_BQ].T.astype(BF16)

    misc = _dot(hb, w1_ref[:, COL_MISC:N_PRE])
    p0 = lax.rem(i * tm, seq)
    pib = p0 + row
    pib = jnp.where(pib >= seq, pib - seq, pib)
    xg = misc + bf_ref[...]
    logf = (jnp.minimum(xg, 0.0) - jnp.log1p(jnp.exp(-jnp.abs(xg)))) * LOG2E
    logf = jnp.where((pib >= PAD) & (lane < N_HEADS), logf, 0.0)

    def pieces(v):
        rep = v + pltpu.roll(v, N_HEADS, 1) + pltpu.roll(v, 2 * N_HEADS, 1)
        hi, mid, lo = _split3(rep)
        return jnp.where(lane < N_HEADS, hi, jnp.where(lane < 2 * N_HEADS, mid, lo))

    def gather_pieces(v):
        return v + pltpu.roll(v, LANES - N_HEADS, 1) + pltpu.roll(v, LANES - 2 * N_HEADS, 1)

    bnd = seq - p0
    tr = lax.broadcasted_iota(jnp.int32, (tm, tm), 0)
    tc = lax.broadcasted_iota(jnp.int32, (tm, tm), 1)
    tri = jnp.where((tc <= tr) & ((tc >= bnd) | (tr < bnd)), 1.0, 0.0).astype(BF16)
    cum = gather_pieces(_dot(tri, pieces(logf).astype(BF16)))

    @pl.when(i == 0)
    def _():
        carry_ref[...] = jnp.zeros_like(carry_ref)

    carry_in = jnp.where(p0 == 0, 0.0, carry_ref[0:1, :])
    cum = jnp.where(lane < N_HEADS, cum + jnp.where(row >= bnd, 0.0, carry_in), 0.0)
    carry_ref[...] = jnp.broadcast_to(cum[tm - 1:tm, :], carry_ref.shape)
    pc = jnp.where(lane == AUG_ONE, 1.0, pieces(cum)).astype(BF16)
    augq = _dot(pc, eq_ref[...])
    augk = _dot(pc, ek_ref[...])

    rb = _dot(hb, w1_ref[:, COL_BQ:COL_CQ])
    for h in range(N_HEADS):
        j, t = divmod(h, 2)
        own = lo_half if t == 0 else jnp.logical_not(lo_half)
        pq = rb[:, LANES * j:LANES * (j + 1)] * SCALE_AB
        pk = rb[:, BRANCH_W + LANES * j:BRANCH_W + LANES * (j + 1)]
        qk_ref[h] = jnp.where(own, pq, augq[:, LANES * h:LANES * (h + 1)]).astype(BF16)
        qk_ref[2 * N_HEADS + h] = jnp.where(own, pk, augk[:, LANES * h:LANES * (h + 1)]).astype(BF16)
    vt_ref[0:BRANCH_W, :] = rb[:, 2 * BRANCH_W:3 * BRANCH_W].T.astype(BF16)

    rc = _dot(hb, w1_ref[:, COL_CQ:COL_MISC])
    cqn = _rms(rc[:, :C_Q_RANK], qng_ref[...]).astype(BF16)
    ckvn = _rms(rc[:, C_Q_RANK:], kvng_ref[...]).astype(BF16)

    def rope(v, c, s1, s2):
        return v * c + pltpu.roll(v, LANES - ROPE_HALF, 1) * s1 + pltpu.roll(v, ROPE_HALF, 1) * s2

    qc = _dot(cqn, wuq_ref[...])
    for h in range(N_HEADS):
        blk = qc[:, LANES * h:LANES * (h + 1)]
        qk_ref[N_HEADS + h] = rope(blk, tab_ref[0], tab_ref[1], tab_ref[2]).astype(BF16)
    kc = _dot(ckvn, wuk_ref[...])
    kr = rope(misc, tab_ref[3], tab_ref[4], tab_ref[5])
    for h in range(N_HEADS):
        qk_ref[3 * N_HEADS + h] = (kc[:, LANES * h:LANES * (h + 1)] + kr).astype(BF16)
    vt_ref[BRANCH_W:2 * BRANCH_W, :] = _dot(ckvn, wuv_ref[...]).T.astype(BF16)


def _const_spec(shape):
    nd = len(shape)
    return pl.BlockSpec(shape, lambda *_: (0,) * nd, pipeline_mode=pl.Buffered(1))


def _pre_call(x, lw, consts, *, seq):
    m = x.shape[0]
    tm = PRE_TM
    n_tiles = m // tm
    period = seq // LANES
    grid_spec = pl.GridSpec(
        grid=(n_tiles,),
        in_specs=[
            pl.BlockSpec((tm, D_MODEL), lambda i: (i, 0)),
            _const_spec((1, D_MODEL)),
            _const_spec((D_MODEL, N_PRE)),
            _const_spec((C_Q_RANK, N_HEADS * LANES)),
            _const_spec((C_KV_RANK, N_HEADS * LANES)),
            _const_spec((C_KV_RANK, BRANCH_W)),
            _const_spec((LANES, N_HEADS * LANES)),
            _const_spec((LANES, N_HEADS * LANES)),
            _const_spec((1, C_Q_RANK)),
            _const_spec((1, C_KV_RANK)),
            _const_spec((1, LANES)),
            pl.BlockSpec((6, tm, LANES), lambda i: (0, i % period, 0)),
        ],
        out_specs=[
            pl.BlockSpec((4 * N_HEADS, tm, LANES), lambda i: (0, i, 0)),
            pl.BlockSpec((2 * BRANCH_W, tm), lambda i: (0, i)),
            pl.BlockSpec((N_HEADS, tm, LANES), lambda i: (0, i, 0)),
            pl.BlockSpec((tm, LANES), lambda i: (i, 0)),
            pl.BlockSpec((LANES, tm), lambda i: (0, i)),
        ],
        scratch_shapes=[pltpu.VMEM((8, LANES), F32)],
    )
    out_shape = [
        jax.ShapeDtypeStruct((4 * N_HEADS, m, LANES), BF16),
        jax.ShapeDtypeStruct((2 * BRANCH_W, m), BF16),
        jax.ShapeDtypeStruct((N_HEADS, m, LANES), BF16),
        jax.ShapeDtypeStruct((m, LANES), BF16),
        jax.ShapeDtypeStruct((LANES, m), BF16),
    ]
    return pl.pallas_call(
        functools.partial(_pre_kernel, tm=tm, seq=seq),
        grid_spec=grid_spec,
        out_shape=out_shape,
        compiler_params=pltpu.CompilerParams(
            dimension_semantics=("arbitrary",), vmem_limit_bytes=VMEM_LIMIT_BYTES),
        name="pre",
    )(x, lw["norm_g"], lw["w1"], lw["wuq"], lw["wuk"], lw["wuv"], consts["eq"], consts["ek"],
      lw["q_norm_g"], lw["kv_norm_g"], lw["b_f"], consts["rope_tab"])


def _swa_kernel(q_ref, k_ref, vt_ref, bias_ref, sink_ref, o_ref, ot_scr, *, seq):
    nb = seq // BLK
    half = N_HEADS // 2 * BLK
    sink = sink_ref[...]
    for n in range(nb):
        q = q_ref[:, n * BLK:(n + 1) * BLK, :].reshape(N_HEADS * BLK, LANES)
        k0 = max(n - 1, 0) * BLK
        k1 = (n + 1) * BLK
        if n == 0:
            bias = bias_ref[2, BLK:, :]
        else:
            bias = bias_ref[0 if n >= 2 else 1]
        s = _dot_nt(k_ref[k0:k1, :], q) + bias
        m = jnp.maximum(jnp.max(s, axis=0, keepdims=True), sink)
        p = jnp.exp2(s - m)
        inv = 1.0 / (jnp.sum(p, axis=0, keepdims=True) + jnp.exp2(sink - m))
        pb = p.astype(BF16)
        for kv in range(A_KV_HEADS):
            pv = _dot(vt_ref[kv * HEAD_DIM:(kv + 1) * HEAD_DIM, k0:k1], pb[:, kv * half:(kv + 1) * half])
            pv = pv * inv[:, kv * half:(kv + 1) * half]
            for g in range(N_HEADS // A_KV_HEADS):
                h = kv * (N_HEADS // A_KV_HEADS) + g
                ot_scr[h * HEAD_DIM:(h + 1) * HEAD_DIM, n * BLK:(n + 1) * BLK] = pv[:, g * BLK:(g + 1) * BLK]
    for c in range(nb):
        o_ref[c * BLK:(c + 1) * BLK, :] = ot_scr[:, c * BLK:(c + 1) * BLK].T.astype(o_ref.dtype)


def _swa_call(qa, ka, vta, bias, sink, *, batch, seq):
    m = ka.shape[0]
    grid_spec = pl.GridSpec(
        grid=(batch,),
        in_specs=[
            pl.BlockSpec((N_HEADS, seq, LANES), lambda b: (0, b, 0)),
            pl.BlockSpec((seq, LANES), lambda b: (b, 0)),
            pl.BlockSpec((LANES, seq), lambda b: (0, b)),
            _const_spec((3, 2 * BLK, N_HEADS * BLK)),
            _const_spec((1, N_HEADS * BLK)),
        ],
        out_specs=pl.BlockSpec((seq, BRANCH_W), lambda b: (b, 0)),
        scratch_shapes=[pltpu.VMEM((BRANCH_W, seq), F32)],
    )
    return pl.pallas_call(
        functools.partial(_swa_kernel, seq=seq),
        grid_spec=grid_spec,
        out_shape=jax.ShapeDtypeStruct((m, BRANCH_W), BF16),
        compiler_params=pltpu.CompilerParams(
            dimension_semantics=("parallel",), vmem_limit_bytes=VMEM_LIMIT_BYTES),
        name="swa",
    )(qa, ka, vta, bias, sink)


def _dense_kernel(q_ref, k_ref, vt_ref, dbias_ref, mbias_ref, mqbias_ref, o_ref, ot_scr, *, seq):
    n_q = (seq - BLK) // Q_TILE

    k_per_q = Q_TILE // K_TILE
    steps = [(0, BLK, 0, BLK, (mqbias_ref, 0), True, True)]
    for i in range(n_q):
        q0 = BLK + Q_TILE * i
        steps.append((q0, Q_TILE, 0, BLK, (mbias_ref, 0), True, False))
        for j in range((i + 1) * k_per_q):
            d = j - i * k_per_q
            steps.append((q0, Q_TILE, BLK + K_TILE * j, K_TILE, (dbias_ref, d * K_TILE) if d >= 0 else None,
                          False, j == (i + 1) * k_per_q - 1))

    def softmax_pv(s, vt, state):
        vt = jnp.concatenate([vt, jnp.ones((SUM_ROWS, vt.shape[1]), BF16)], axis=0)
        mc = jnp.max(s, axis=0, keepdims=True)
        if state is None:
            return mc, _dot(vt, jnp.exp2(s - mc).astype(BF16))
        m, acc = state
        m_new = jnp.maximum(m, mc)
        alpha = jnp.exp2(m - m_new)
        return m_new, alpha * acc + _dot(vt, jnp.exp2(s - m_new).astype(BF16))

    def head_body(hg, carry):
        heads = [hg * HEADS_PER_ITER + c for c in range(HEADS_PER_ITER)]
        rows = [pl.ds(pl.multiple_of(h * HEAD_DIM, HEAD_DIM), HEAD_DIM) for h in heads]

        def scores(h, st):
            q0, qn, c0, cn = st[:4]
            return _dot_nt(k_ref[h, c0:c0 + cn, :], q_ref[h, q0:q0 + qn, :])

        ahead = [[scores(h, st) for st in steps[:LOOKAHEAD]] for h in heads]
        state = [None] * HEADS_PER_ITER
        for t, (q0, qn, c0, cn, bias, first, last) in enumerate(steps):
            for c, h in enumerate(heads):
                s = ahead[c].pop(0)
                if t + LOOKAHEAD < len(steps):
                    ahead[c].append(scores(h, steps[t + LOOKAHEAD]))
                if bias is not None:
                    s = s + bias[0][bias[1]:bias[1] + cn, :]
                state[c] = softmax_pv(s, vt_ref[rows[c], c0:c0 + cn], None if first else state[c])
                if last:
                    acc = state[c][1]
                    ot_scr[rows[c], q0:q0 + qn] = acc[:HEAD_DIM] / acc[HEAD_DIM:HEAD_DIM + 1]
        return carry

    lax.fori_loop(0, N_HEADS // HEADS_PER_ITER, head_body, 0)
    for c in range(seq // BLK):
        o_ref[0, c * BLK:(c + 1) * BLK, :] = ot_scr[:, c * BLK:(c + 1) * BLK].T.astype(o_ref.dtype)


def _dense_call(qk, vt, consts, *, batch, seq):
    m = qk.shape[1]
    grid_spec = pl.GridSpec(
        grid=(batch, 2),
        in_specs=[
            pl.BlockSpec((N_HEADS, seq, LANES), lambda b, x: (x, b, 0)),
            pl.BlockSpec((N_HEADS, seq, LANES), lambda b, x: (2 + x, b, 0)),
            pl.BlockSpec((BRANCH_W, seq), lambda b, x: (x, b)),
            _const_spec((Q_TILE, Q_TILE)),
            _const_spec((BLK, Q_TILE)),
            _const_spec((BLK, BLK)),
        ],
        out_specs=pl.BlockSpec((1, seq, BRANCH_W), lambda b, x: (x, b, 0)),
        scratch_shapes=[pltpu.VMEM((BRANCH_W, seq), F32)],
    )
    return pl.pallas_call(
        functools.partial(_dense_kernel, seq=seq),
        grid_spec=grid_spec,
        out_shape=jax.ShapeDtypeStruct((2, m, BRANCH_W), BF16),
        compiler_params=pltpu.CompilerParams(
            dimension_semantics=("parallel", "parallel"), vmem_limit_bytes=VMEM_LIMIT_BYTES),
        name="dense",
    )(qk, qk, vt, consts["dbias"], consts["mbias"], consts["mqbias"])


def _post_kernel(x_ref, oa_ref, obc_ref, g_ref, wz_ref, wg_ref, wbr_ref, wout_ref, fg_ref, out_ref, *, final):
    x = x_ref[...]
    hb = _rms(x, g_ref[...]).astype(BF16)
    y = None
    for i in range(N_BRANCH):
        o = (oa_ref[...] if i == 0 else obc_ref[i - 1]).astype(F32)
        z = _dot(hb, wz_ref[:, BRANCH_W * i:BRANCH_W * (i + 1)])
        u = (o * (z * jax.nn.sigmoid(z))).astype(BF16)
        t = _dot(u, wbr_ref[i])
        gate = jax.nn.sigmoid(_dot(hb, wg_ref[:, D_MODEL * i:D_MODEL * (i + 1)]))
        y = gate * t if y is None else y + gate * t
    out = x + _dot(y.astype(BF16), wout_ref[...])
    if final:
        out = _rms(out, fg_ref[...])
    out_ref[...] = out


def _post_call(x, oa, obc, lw, final_g, *, final):
    m = x.shape[0]
    tm = POST_TM
    grid_spec = pl.GridSpec(
        grid=(m // tm,),
        in_specs=[
            pl.BlockSpec((tm, D_MODEL), lambda i: (i, 0)),
            pl.BlockSpec((tm, BRANCH_W), lambda i: (i, 0)),
            pl.BlockSpec((2, tm, BRANCH_W), lambda i: (0, i, 0)),
            _const_spec((1, D_MODEL)),
            _const_spec((D_MODEL, N_BRANCH * BRANCH_W)),
            _const_spec((D_MODEL, N_BRANCH * D_MODEL)),
            _const_spec((N_BRANCH, BRANCH_W, D_MODEL)),
            _const_spec((D_MODEL, D_MODEL)),
            _const_spec((1, D_MODEL)),
        ],
        out_specs=pl.BlockSpec((tm, D_MODEL), lambda i: (i, 0)),
    )
    return pl.pallas_call(
        functools.partial(_post_kernel, final=final),
        grid_spec=grid_spec,
        out_shape=jax.ShapeDtypeStruct((m, D_MODEL), F32),
        input_output_aliases={0: 0},
        compiler_params=pltpu.CompilerParams(
            dimension_semantics=("parallel",), vmem_limit_bytes=VMEM_LIMIT_BYTES),
        name="post",
    )(x, oa, obc, lw["norm_g"], lw["wz"], lw["wg"], lw["w_br"], lw["w_out"], final_g)


def _rope_tables(seq, tm):
    pos = (jnp.arange(seq) - PAD).astype(F32)
    inv = ROPE_THETA ** (-jnp.arange(ROPE_HALF, dtype=F32) / ROPE_HALF)
    ang = pos[:, None] * inv[None, :]
    cos, sin = jnp.cos(ang), jnp.sin(ang)
    zero = jnp.zeros_like(cos)

    def lanes(nope, a, b):
        return jnp.concatenate([jnp.full((seq, C_NOPE), nope, F32), a, b,
                                jnp.zeros((seq, LANES - C_NOPE - C_ROPE), F32)], axis=1)

    tabs = jnp.stack([lanes(1.0, cos, cos) * SCALE_C, lanes(0.0, -sin, zero) * SCALE_C,
                      lanes(0.0, zero, sin) * SCALE_C,
                      lanes(0.0, cos, cos), lanes(0.0, -sin, zero), lanes(0.0, zero, sin)])
    return jnp.tile(tabs, (1, tm // LANES, 1))


def _aug_matrices():
    eq = np.zeros((LANES, N_HEADS * LANES), np.float32)
    ek = np.zeros((LANES, N_HEADS * LANES), np.float32)
    for h in range(N_HEADS):
        a = LANES * h + (HEAD_DIM if h % 2 == 0 else 0)
        for p in range(3):
            eq[p * N_HEADS + h, a + p] = 1.0
            eq[AUG_ONE, a + 3 + p] = 1.0
            ek[AUG_ONE, a + p] = 1.0
            ek[p * N_HEADS + h, a + 3 + p] = -1.0
    return jnp.asarray(eq, BF16), jnp.asarray(ek, BF16)


def _dense_biases():
    kk = np.arange(Q_TILE)[:, None]
    qq = np.arange(Q_TILE)[None, :]
    dbias = np.where(kk <= qq, 0.0, NEG).astype(np.float32)
    mbias = np.where(np.arange(BLK)[:, None] >= PAD, 0.0, NEG) * np.ones((1, Q_TILE))
    km = np.arange(BLK)[:, None]
    qm = np.arange(BLK)[None, :]
    mqbias = np.where((km >= PAD) & (km <= qm), 0.0, NEG)
    return jnp.asarray(dbias, F32), jnp.asarray(mbias, F32), jnp.asarray(mqbias, F32)


def _swa_biases():
    slopes = 2.0 ** (-8.0 * (jnp.arange(N_HEADS, dtype=F32) + 1.0) / N_HEADS)
    s = np.arange(2 * BLK)[:, None]
    t = np.arange(BLK)[None, :]
    dist = BLK + t - s
    window = (dist >= 0) & (dist < BLK)
    tabs = []
    for mask in (window, window & (s >= PAD), window & (s >= BLK + PAD)):
        per_head = [jnp.where(mask, -slopes[h] * dist.astype(np.float32) * LOG2E, NEG) for h in range(N_HEADS)]
        tabs.append(jnp.concatenate(per_head, axis=1))
    return jnp.stack(tabs).astype(F32)


def _layer_weights(l, norm_g, w_in, b_f, sinks, q_norm_g, kv_norm_g, w_uq, w_ukv, w_br, w_out):
    w = w_in[l]
    splits = np.cumsum([0, 512, 128, 128, 512, 512, 512, 8, 256, 128, 32, 1536, 3072])
    aq, ak, av, bq, bk, bv, bfl, cq, ckv, ckr, z, g = [w[:, splits[i]:splits[i + 1]] for i in range(12)]
    aq = aq.reshape(D_MODEL, A_KV_HEADS, N_HEADS // A_KV_HEADS, HEAD_DIM)
    aq = jnp.transpose(aq, (0, 2, 1, 3)).reshape(D_MODEL, N_HEADS * HEAD_DIM)
    misc = jnp.concatenate([bfl, jnp.zeros((D_MODEL, MISC_ROPE - N_HEADS), F32), ckr,
                            jnp.zeros((D_MODEL, LANES - MISC_ROPE - C_ROPE), F32)], axis=1)
    w1 = jnp.concatenate([aq, ak, av, bq, bk, bv, cq, ckv, misc], axis=1).astype(BF16)
    wuq = jnp.pad(w_uq[l].reshape(C_Q_RANK, N_HEADS, C_NOPE + C_ROPE),
                  ((0, 0), (0, 0), (0, LANES - C_NOPE - C_ROPE))).reshape(C_Q_RANK, N_HEADS * LANES)
    wukv = w_ukv[l].reshape(C_KV_RANK, N_HEADS, 2 * HEAD_DIM)
    wuk = jnp.pad(wukv[:, :, :C_NOPE], ((0, 0), (0, 0), (0, LANES - C_NOPE))).reshape(C_KV_RANK, N_HEADS * LANES)
    wuv = wukv[:, :, C_NOPE:].reshape(C_KV_RANK, BRANCH_W)
    bf_row = jnp.pad(b_f[l], (0, LANES - N_HEADS)).reshape(1, LANES)
    sink_row = jnp.repeat(sinks[l] * LOG2E, BLK).reshape(1, N_HEADS * BLK)
    return {
        "norm_g": norm_g[l].reshape(1, D_MODEL), "w1": w1, "wuq": wuq.astype(BF16), "wuk": wuk.astype(BF16),
        "wuv": wuv.astype(BF16), "q_norm_g": q_norm_g[l].reshape(1, C_Q_RANK),
        "kv_norm_g": kv_norm_g[l].reshape(1, C_KV_RANK), "b_f": bf_row, "sink": sink_row,
        "wz": z.astype(BF16), "wg": g.astype(BF16), "w_br": w_br[l].astype(BF16), "w_out": w_out[l].astype(BF16),
    }


def kernel(x, meta_tokens, norm_g, w_in, b_f, sinks, q_norm_g, kv_norm_g, w_uq, w_ukv, w_br, w_out, final_norm_g):
    batch, n_real, _ = x.shape
    seq = BLK + n_real
    depth = w_in.shape[0]
    assert (seq - BLK) % Q_TILE == 0 and (batch * seq) % PRE_TM == 0 and (batch * seq) % POST_TM == 0
    h = jnp.concatenate([jnp.zeros((batch, PAD, D_MODEL), x.dtype),
                         jnp.broadcast_to(meta_tokens[None].astype(x.dtype), (batch, N_META, D_MODEL)),
                         x], axis=1).reshape(batch * seq, D_MODEL)
    eq, ek = _aug_matrices()
    dbias, mbias, mqbias = _dense_biases()
    consts = {"eq": eq, "ek": ek, "rope_tab": _rope_tables(seq, PRE_TM),
              "dbias": dbias, "mbias": mbias, "mqbias": mqbias}
    swa_bias = _swa_biases()
    final_g = final_norm_g.reshape(1, D_MODEL)
    for l in range(depth):
        lw = _layer_weights(l, norm_g, w_in, b_f, sinks, q_norm_g, kv_norm_g, w_uq, w_ukv, w_br, w_out)
        qk, vt, qa, ka, vta = _pre_call(h, lw, consts, seq=seq)
        oa = _swa_call(qa, ka, vta, swa_bias, lw["sink"], batch=batch, seq=seq)
        obc = _dense_call(qk, vt, consts, batch=batch, seq=seq)
        h = _post_call(h, oa, obc, lw, final_g, final=(l == depth - 1))
    return h.reshape(batch, seq, D_MODEL)[:, BLK:]
```

```python
import functools
import math

import jax
import jax.numpy as jnp
import numpy as np
from jax import lax
from jax.experimental import pallas as pl
from jax.experimental.pallas import tpu as pltpu

D_MODEL = 1024
N_META = 16
BLK = 128
PAD = BLK - N_META
HEAD_DIM = 64
N_HEADS = 8
A_KV_HEADS = 2
C_Q_RANK = 256
C_KV_RANK = 128
C_NOPE = 64
C_ROPE = 32
ROPE_HALF = C_ROPE // 2
ROPE_THETA = 10000.0
N_BRANCH = 3
BRANCH_W = 512
EPS = 1e-6
NEG = -1e30
LOG2E = math.log2(math.e)

LANES = 128
Q_SUB = 512
K_TILE = 512
CHAINS_PER_GROUP = 1
LOOKAHEAD = 2
HEADS_PER_ITER = 2
SUM_ROWS = 16
PRE_TM = 512
POST_TM = 1024
VMEM_LIMIT_BYTES = 56 * 1024 * 1024

SCALE_AB = HEAD_DIM ** -0.5 * LOG2E
SCALE_C = (C_NOPE + C_ROPE) ** -0.5 * LOG2E

COL_AQ = 0
COL_AK = 512
COL_AV = 640
COL_BQ = 768
COL_BK = 1280
COL_BV = 1792
COL_CQ = 2304
COL_CKV = 2560
COL_MISC = 2688
N_PRE = 2816
MISC_ROPE = 64
AUG_ONE = 24

BF16 = jnp.bfloat16
F32 = jnp.float32


def _dot(a, b):
    return jnp.dot(a, b, preferred_element_type=F32)


def _dot_nt(a, b):
    return lax.dot_general(a, b, (((1,), (1,)), ((), ())), preferred_element_type=F32)


def _rms(x, g):
    return x * lax.rsqrt(jnp.mean(x * x, axis=-1, keepdims=True) + EPS) * g


def _split3(x):
    hi = x.astype(BF16).astype(F32)
    r = x - hi
    mid = r.astype(BF16).astype(F32)
    lo = (r - mid).astype(BF16).astype(F32)
    return hi, mid, lo


def _pre_kernel(x_ref, g_ref, w1_ref, wuq_ref, wuk_ref, wuv_ref, eq_ref, ek_ref, qng_ref, kvng_ref,
                bf_ref, tab_ref, qk_ref, vt_ref, qa_ref, ka_ref, vta_ref, carry_ref, *, tm, seq):
    i = pl.program_id(0)
    hb = _rms(x_ref[...], g_ref[...]).astype(BF16)
    lane = lax.broadcasted_iota(jnp.int32, (tm, LANES), 1)
    row = lax.broadcasted_iota(jnp.int32, (tm, LANES), 0)
    lo_half = lane < HEAD_DIM

    misc = _dot(hb, w1_ref[:, COL_MISC:N_PRE])
    rc = _dot(hb, w1_ref[:, COL_CQ:COL_MISC])
    ra = _dot(hb, w1_ref[:, COL_AQ:COL_BQ])

    for j in range(N_HEADS // 2):
        pq = ra[:, LANES * j:LANES * (j + 1)] * SCALE_AB
        qa_ref[j] = jnp.where(lo_half, pq, 0.0).astype(BF16)
        qa_ref[j + N_HEADS // 2] = jnp.where(lo_half, 0.0, pq).astype(BF16)
    ka_ref[...] = ra[:, COL_AK:COL_AV].astype(BF16)
    vta_ref[...] = ra[:, COL_AV:COL_BQ].T.astype(BF16)

    p0 = lax.rem(i * tm, seq)
    pib = p0 + row
    pib = jnp.where(pib >= seq, pib - seq, pib)
    xg = misc + bf_ref[...]
    logf = (jnp.minimum(xg, 0.0) - jnp.log1p(jnp.exp(-jnp.abs(xg)))) * LOG2E
    logf = jnp.where((pib >= PAD) & (lane < N_HEADS), logf, 0.0)

    def pieces(v):
        rep = v + pltpu.roll(v, N_HEADS, 1) + pltpu.roll(v, 2 * N_HEADS, 1)
        hi, mid, lo = _split3(rep)
        return jnp.where(lane < N_HEADS, hi, jnp.where(lane < 2 * N_HEADS, mid, lo))

    def gather_pieces(v):
        return v + pltpu.roll(v, LANES - N_HEADS, 1) + pltpu.roll(v, LANES - 2 * N_HEADS, 1)

    bnd = seq - p0
    tr = lax.broadcasted_iota(jnp.int32, (tm, tm), 0)
    tc = lax.broadcasted_iota(jnp.int32, (tm, tm), 1)
    tri = jnp.where((tc <= tr) & ((tc >= bnd) | (tr < bnd)), 1.0, 0.0).astype(BF16)
    cum = gather_pieces(_dot(tri, pieces(logf).astype(BF16)))

    def rope(v, c, s1, s2):
        return v * c + pltpu.roll(v, LANES - ROPE_HALF, 1) * s1 + pltpu.roll(v, ROPE_HALF, 1) * s2

    cqn = _rms(rc[:, :C_Q_RANK], qng_ref[...]).astype(BF16)
    ckvn = _rms(rc[:, C_Q_RANK:], kvng_ref[...]).astype(BF16)
    qc = _dot(cqn, wuq_ref[...])
    for h in range(N_HEADS):
        blk = qc[:, LANES * h:LANES * (h + 1)]
        qk_ref[N_HEADS + h] = rope(blk, tab_ref[0], tab_ref[1], tab_ref[2]).astype(BF16)

    rb = _dot(hb, w1_ref[:, COL_BQ:COL_CQ])

    @pl.when(i == 0)
    def _():
        carry_ref[...] = jnp.zeros_like(carry_ref)

    carry_in = jnp.where(p0 == 0, 0.0, carry_ref[0:1, :])
    cum = jnp.where(lane < N_HEADS, cum + jnp.where(row >= bnd, 0.0, carry_in), 0.0)
    carry_ref[...] = jnp.broadcast_to(cum[tm - 1:tm, :], carry_ref.shape)
    pc = jnp.where(lane == AUG_ONE, 1.0, pieces(cum)).astype(BF16)
    augq = _dot(pc, eq_ref[...])
    augk = _dot(pc, ek_ref[...])

    for h in range(N_HEADS):
        j, t = divmod(h, 2)
        own = lo_half if t == 0 else jnp.logical_not(lo_half)
        pq = rb[:, LANES * j:LANES * (j + 1)] * SCALE_AB
        pk = rb[:, BRANCH_W + LANES * j:BRANCH_W + LANES * (j + 1)]
        qk_ref[h] = jnp.where(own, pq, augq[:, LANES * h:LANES * (h + 1)]).astype(BF16)
        qk_ref[2 * N_HEADS + h] = jnp.where(own, pk, augk[:, LANES * h:LANES * (h + 1)]).astype(BF16)
    vt_ref[0:BRANCH_W, :] = rb[:, 2 * BRANCH_W:3 * BRANCH_W].T.astype(BF16)

    kc = _dot(ckvn, wuk_ref[...])
    kr = rope(misc, tab_ref[3], tab_ref[4], tab_ref[5])
    for h in range(N_HEADS):
        qk_ref[3 * N_HEADS + h] = jnp.where(lo_half, kc[:, LANES * h:LANES * (h + 1)], kr).astype(BF16)
    vt_ref[BRANCH_W:2 * BRANCH_W, :] = _dot(ckvn, wuv_ref[...]).T.astype(BF16)


def _const_spec(shape):
    nd = len(shape)
    return pl.BlockSpec(shape, lambda *_: (0,) * nd, pipeline_mode=pl.Buffered(1))


def _pre_call(x, lw, consts, *, seq):
    m = x.shape[0]
    tm = PRE_TM
    n_tiles = m // tm
    period = seq // LANES
    grid_spec = pl.GridSpec(
        grid=(n_tiles,),
        in_specs=[
            pl.BlockSpec((tm, D_MODEL), lambda i: (i, 0)),
            _const_spec((1, D_MODEL)),
            _const_spec((D_MODEL, N_PRE)),
            _const_spec((C_Q_RANK, N_HEADS * LANES)),
            _const_spec((C_KV_RANK, N_HEADS * LANES)),
            _const_spec((C_KV_RANK, BRANCH_W)),
            _const_spec((LANES, N_HEADS * LANES)),
            _const_spec((LANES, N_HEADS * LANES)),
            _const_spec((1, C_Q_RANK)),
            _const_spec((1, C_KV_RANK)),
            _const_spec((1, LANES)),
            pl.BlockSpec((6, tm, LANES), lambda i: (0, i % period, 0)),
        ],
        out_specs=[
            pl.BlockSpec((4 * N_HEADS, tm, LANES), lambda i: (0, i, 0)),
            pl.BlockSpec((2 * BRANCH_W, tm), lambda i: (0, i)),
            pl.BlockSpec((N_HEADS, tm, LANES), lambda i: (0, i, 0)),
            pl.BlockSpec((tm, LANES), lambda i: (i, 0)),
            pl.BlockSpec((LANES, tm), lambda i: (0, i)),
        ],
        scratch_shapes=[pltpu.VMEM((8, LANES), F32)],
    )
    out_shape = [
        jax.ShapeDtypeStruct((4 * N_HEADS, m, LANES), BF16),
        jax.ShapeDtypeStruct((2 * BRANCH_W, m), BF16),
        jax.ShapeDtypeStruct((N_HEADS, m, LANES), BF16),
        jax.ShapeDtypeStruct((m, LANES), BF16),
        jax.ShapeDtypeStruct((LANES, m), BF16),
    ]
    return pl.pallas_call(
        functools.partial(_pre_kernel, tm=tm, seq=seq),
        grid_spec=grid_spec,
        out_shape=out_shape,
        compiler_params=pltpu.CompilerParams(
            dimension_semantics=("arbitrary",), vmem_limit_bytes=VMEM_LIMIT_BYTES),
        name="pre",
    )(x, lw["norm_g"], lw["w1"], lw["wuq"], lw["wuk"], lw["wuv"], consts["eq"], consts["ek"],
      lw["q_norm_g"], lw["kv_norm_g"], lw["b_f"], consts["rope_tab"])


def _swa_kernel(q_ref, k_ref, vt_ref, bias_ref, sink_ref, o_ref, ot_scr, *, seq):
    nb = seq // BLK
    half = N_HEADS // 2 * BLK
    sink = sink_ref[...]
    def scores(n):
        q = q_ref[:, n * BLK:(n + 1) * BLK, :].reshape(N_HEADS * BLK, LANES)
        return _dot_nt(k_ref[max(n - 1, 0) * BLK:(n + 1) * BLK, :], q)

    s_next = scores(0)
    for n in range(nb):
        k0 = max(n - 1, 0) * BLK
        k1 = (n + 1) * BLK
        if n == 0:
            bias = bias_ref[2, BLK:, :]
        else:
            bias = bias_ref[0 if n >= 2 else 1]
        s = jnp.minimum(s_next, -NEG) + bias
        if n + 1 < nb:
            s_next = scores(n + 1)
        m = jnp.maximum(jnp.max(s, axis=0, keepdims=True), sink)
        pb = jnp.exp2(s - m).astype(BF16)
        sink_p = jnp.exp2(sink - m)
        ones = jnp.ones((SUM_ROWS, k1 - k0), BF16)
        for kv in range(A_KV_HEADS):
            vt = jnp.concatenate([vt_ref[kv * HEAD_DIM:(kv + 1) * HEAD_DIM, k0:k1], ones], axis=0)
            pv = _dot(vt, pb[:, kv * half:(kv + 1) * half])
            pv = pv[:HEAD_DIM] / (pv[HEAD_DIM:HEAD_DIM + 1] + sink_p[:, kv * half:(kv + 1) * half])
            for g in range(N_HEADS // A_KV_HEADS):
                h = kv * (N_HEADS // A_KV_HEADS) + g
                ot_scr[h * HEAD_DIM:(h + 1) * HEAD_DIM, n * BLK:(n + 1) * BLK] = pv[:, g * BLK:(g + 1) * BLK]
    for c in range(nb):
        o_ref[c * BLK:(c + 1) * BLK, :] = ot_scr[:, c * BLK:(c + 1) * BLK].T.astype(o_ref.dtype)


def _swa_call(qa, ka, vta, bias, sink, *, batch, seq):
    m = ka.shape[0]
    grid_spec = pl.GridSpec(
        grid=(batch,),
        in_specs=[
            pl.BlockSpec((N_HEADS, seq, LANES), lambda b: (0, b, 0)),
            pl.BlockSpec((seq, LANES), lambda b: (b, 0)),
            pl.BlockSpec((LANES, seq), lambda b: (0, b)),
            _const_spec((3, 2 * BLK, N_HEADS * BLK)),
            _const_spec((1, N_HEADS * BLK)),
        ],
        out_specs=pl.BlockSpec((seq, BRANCH_W), lambda b: (b, 0)),
        scratch_shapes=[pltpu.VMEM((BRANCH_W, seq), F32)],
    )
    return pl.pallas_call(
        functools.partial(_swa_kernel, seq=seq),
        grid_spec=grid_spec,
        out_shape=jax.ShapeDtypeStruct((m, BRANCH_W), BF16),
        compiler_params=pltpu.CompilerParams(
            dimension_semantics=("parallel",), vmem_limit_bytes=VMEM_LIMIT_BYTES),
        name="swa",
    )(qa, ka, vta, bias, sink)


def _dense_kernel(q_ref, k_ref, vt_ref, dcap_ref, mqcap_ref, o_ref, ot_scr, *, seq):
    def head_chains(c):
        groups = [[(c, 0, BLK, [([(0, BLK)], [(mqcap_ref, 0, BLK)])])]]
        for u in range((seq - BLK) // Q_SUB):
            q0 = BLK + Q_SUB * u
            n_full = u * Q_SUB // K_TILE
            steps = [([(BLK + K_TILE * j, K_TILE)], None) for j in range(n_full)]
            rows = q0 + Q_SUB - (BLK + K_TILE * n_full)
            steps.append(([(0, BLK), (BLK + K_TILE * n_full, rows)],
                          [(dcap_ref, 0, BLK), (dcap_ref, BLK + K_TILE - rows, rows)]))
            if u % CHAINS_PER_GROUP == 0:
                groups.append([])
            groups[-1].append((c, q0, Q_SUB, steps))
        return groups

    def gather(ref_rows, ranges, axis):
        parts = [ref_rows(a, n) for a, n in ranges]
        return parts[0] if len(parts) == 1 else jnp.concatenate(parts, axis=axis)

    def softmax_pv(s, vt, state):
        vt = jnp.concatenate([vt, jnp.ones((SUM_ROWS, vt.shape[1]), BF16)], axis=0)
        mc = jnp.max(s, axis=0, keepdims=True)
        if state is None:
            return mc, _dot(vt, jnp.exp2(s - mc).astype(BF16))
        m, acc = state
        m_new = jnp.maximum(m, mc)
        alpha = jnp.exp2(m - m_new)
        return m_new, alpha * acc + _dot(vt, jnp.exp2(s - m_new).astype(BF16))

    def head_body(hg, carry):
        heads = [hg * HEADS_PER_ITER + c for c in range(HEADS_PER_ITER)]
        rows = [pl.ds(pl.multiple_of(h * HEAD_DIM, HEAD_DIM), HEAD_DIM) for h in heads]
        per_head = [head_chains(c) for c in range(HEADS_PER_ITER)]
        slots = []
        for gi in range(len(per_head[0])):
            group = [ch for groups in per_head for ch in groups[gi]]
            for t in range(len(group[0][3])):
                slots += [(ch, t) for ch in group]

        def scores(slot):
            (c, q0, qn, steps), t = slot
            keys = gather(lambda a, n: k_ref[heads[c], a:a + n, :], steps[t][0], 0)
            return _dot_nt(keys, q_ref[heads[c], q0:q0 + qn, :])

        ahead = [scores(sl) for sl in slots[:LOOKAHEAD]]
        state = {}
        for n, ((c, q0, qn, steps), t) in enumerate(slots):
            s = ahead.pop(0)
            if n + LOOKAHEAD < len(slots):
                ahead.append(scores(slots[n + LOOKAHEAD]))
            key_ranges, cap_ranges = steps[t]
            if cap_ranges is not None:
                cap = [ref[a:a + n, :] for ref, a, n in cap_ranges]
                s = jnp.minimum(s, cap[0] if len(cap) == 1 else jnp.concatenate(cap, axis=0))
            vt = gather(lambda a, n: vt_ref[rows[c], a:a + n], key_ranges, 1)
            state[c, q0] = softmax_pv(s, vt, state.get((c, q0)))
            if t == len(steps) - 1:
                acc = state.pop((c, q0))[1]
                ot_scr[rows[c], q0:q0 + qn] = acc[:HEAD_DIM] / acc[HEAD_DIM:HEAD_DIM + 1]
        return carry

    lax.fori_loop(0, N_HEADS // HEADS_PER_ITER, head_body, 0)
    for c in range(seq // BLK):
        o_ref[0, c * BLK:(c + 1) * BLK, :] = ot_scr[:, c * BLK:(c + 1) * BLK].T.astype(o_ref.dtype)


def _dense_call(qk, vt, consts, *, batch, seq):
    m = qk.shape[1]
    grid_spec = pl.GridSpec(
        grid=(batch, 2),
        in_specs=[
            pl.BlockSpec((N_HEADS, seq, LANES), lambda b, x: (x, b, 0)),
            pl.BlockSpec((N_HEADS, seq, LANES), lambda b, x: (2 + x, b, 0)),
            pl.BlockSpec((BRANCH_W, seq), lambda b, x: (x, b)),
            _const_spec((BLK + K_TILE, Q_SUB)),
            _const_spec((BLK, BLK)),
        ],
        out_specs=pl.BlockSpec((1, seq, BRANCH_W), lambda b, x: (x, b, 0)),
        scratch_shapes=[pltpu.VMEM((BRANCH_W, seq), F32)],
    )
    return pl.pallas_call(
        functools.partial(_dense_kernel, seq=seq),
        grid_spec=grid_spec,
        out_shape=jax.ShapeDtypeStruct((2, m, BRANCH_W), BF16),
        compiler_params=pltpu.CompilerParams(
            dimension_semantics=("parallel", "parallel"), vmem_limit_bytes=VMEM_LIMIT_BYTES),
        name="dense",
    )(qk, qk, vt, consts["dcap"], consts["mqcap"])


def _post_kernel(x_ref, oa_ref, obc_ref, g_ref, wz_ref, wg_ref, wbr_ref, wout_ref, fg_ref, out_ref, *, final):
    x = x_ref[...]
    hb = _rms(x, g_ref[...]).astype(BF16)
    y = None
    for i in range(N_BRANCH):
        o = (oa_ref[...] if i == 0 else obc_ref[i - 1]).astype(F32)
        z = _dot(hb, wz_ref[:, BRANCH_W * i:BRANCH_W * (i + 1)])
        gate = jax.nn.sigmoid(_dot(hb, wg_ref[:, D_MODEL * i:D_MODEL * (i + 1)]))
        u = (o * (z * jax.nn.sigmoid(z))).astype(BF16)
        t = _dot(u, wbr_ref[i])
        y = gate * t if y is None else y + gate * t
    out = x + _dot(y.astype(BF16), wout_ref[...])
    if final:
        out = _rms(out, fg_ref[...])
    out_ref[...] = out


def _post_call(x, oa, obc, lw, final_g, *, final):
    m = x.shape[0]
    tm = POST_TM
    grid_spec = pl.GridSpec(
        grid=(m // tm,),
        in_specs=[
            pl.BlockSpec((tm, D_MODEL), lambda i: (i, 0)),
            pl.BlockSpec((tm, BRANCH_W), lambda i: (i, 0)),
            pl.BlockSpec((2, tm, BRANCH_W), lambda i: (0, i, 0)),
            _const_spec((1, D_MODEL)),
            _const_spec((D_MODEL, N_BRANCH * BRANCH_W)),
            _const_spec((D_MODEL, N_BRANCH * D_MODEL)),
            _const_spec((N_BRANCH, BRANCH_W, D_MODEL)),
            _const_spec((D_MODEL, D_MODEL)),
            _const_spec((1, D_MODEL)),
        ],
        out_specs=pl.BlockSpec((tm, D_MODEL), lambda i: (i, 0)),
    )
    return pl.pallas_call(
        functools.partial(_post_kernel, final=final),
        grid_spec=grid_spec,
        out_shape=jax.ShapeDtypeStruct((m, D_MODEL), F32),
        input_output_aliases={0: 0},
        compiler_params=pltpu.CompilerParams(
            dimension_semantics=("parallel",), vmem_limit_bytes=VMEM_LIMIT_BYTES),
        name="post",
    )(x, oa, obc, lw["norm_g"], lw["wz"], lw["wg"], lw["w_br"], lw["w_out"], final_g)


def _rope_tables(seq, tm):
    pos = (jnp.arange(seq) - PAD).astype(F32)
    inv = ROPE_THETA ** (-jnp.arange(ROPE_HALF, dtype=F32) / ROPE_HALF)
    ang = pos[:, None] * inv[None, :]
    cos, sin = jnp.cos(ang), jnp.sin(ang)
    zero = jnp.zeros_like(cos)

    def lanes(nope, a, b):
        return jnp.concatenate([jnp.full((seq, C_NOPE), nope, F32), a, b,
                                jnp.zeros((seq, LANES - C_NOPE - C_ROPE), F32)], axis=1)

    tabs = jnp.stack([lanes(1.0, cos, cos) * SCALE_C, lanes(0.0, -sin, zero) * SCALE_C,
                      lanes(0.0, zero, sin) * SCALE_C,
                      lanes(0.0, cos, cos), lanes(0.0, -sin, zero), lanes(0.0, zero, sin)])
    return jnp.tile(tabs, (1, tm // LANES, 1))


def _aug_matrices():
    eq = np.zeros((LANES, N_HEADS * LANES), np.float32)
    ek = np.zeros((LANES, N_HEADS * LANES), np.float32)
    for h in range(N_HEADS):
        a = LANES * h + (HEAD_DIM if h % 2 == 0 else 0)
        for p in range(3):
            eq[p * N_HEADS + h, a + p] = 1.0
            eq[AUG_ONE, a + 3 + p] = 1.0
            ek[AUG_ONE, a + p] = 1.0
            ek[p * N_HEADS + h, a + 3 + p] = -1.0
    return jnp.asarray(eq, BF16), jnp.asarray(ek, BF16)


def _dense_caps():
    meta_ok = (np.arange(BLK)[:, None] >= PAD) & np.ones((1, Q_SUB), bool)
    kk = np.arange(K_TILE)[:, None] - (K_TILE - Q_SUB)
    diag_ok = kk <= np.arange(Q_SUB)[None, :]
    dcap = np.where(np.concatenate([meta_ok, diag_ok]), -NEG, NEG)
    km = np.arange(BLK)[:, None]
    mqcap = np.where((km >= PAD) & (km <= np.arange(BLK)[None, :]), -NEG, NEG)
    return jnp.asarray(dcap, F32), jnp.asarray(mqcap, F32)


def _swa_biases():
    slopes = 2.0 ** (-8.0 * (jnp.arange(N_HEADS, dtype=F32) + 1.0) / N_HEADS)
    s = np.arange(2 * BLK)[:, None]
    t = np.arange(BLK)[None, :]
    dist = BLK + t - s
    window = (dist >= 0) & (dist < BLK)
    tabs = []
    for mask in (window, window & (s >= PAD), window & (s >= BLK + PAD)):
        per_head = [jnp.where(mask, -slopes[h] * dist.astype(np.float32) * LOG2E, NEG) for h in range(N_HEADS)]
        tabs.append(jnp.concatenate(per_head, axis=1))
    return jnp.stack(tabs).astype(F32)


def _layer_weights(l, norm_g, w_in, b_f, sinks, q_norm_g, kv_norm_g, w_uq, w_ukv, w_br, w_out):
    w = w_in[l]
    splits = np.cumsum([0, 512, 128, 128, 512, 512, 512, 8, 256, 128, 32, 1536, 3072])
    aq, ak, av, bq, bk, bv, bfl, cq, ckv, ckr, z, g = [w[:, splits[i]:splits[i + 1]] for i in range(12)]
    aq = aq.reshape(D_MODEL, A_KV_HEADS, N_HEADS // A_KV_HEADS, HEAD_DIM)
    aq = jnp.transpose(aq, (0, 2, 1, 3)).reshape(D_MODEL, N_HEADS * HEAD_DIM)
    misc = jnp.concatenate([bfl, jnp.zeros((D_MODEL, MISC_ROPE - N_HEADS), F32), ckr,
                            jnp.zeros((D_MODEL, LANES - MISC_ROPE - C_ROPE), F32)], axis=1)
    w1 = jnp.concatenate([aq, ak, av, bq, bk, bv, cq, ckv, misc], axis=1).astype(BF16)
    wuq = jnp.pad(w_uq[l].reshape(C_Q_RANK, N_HEADS, C_NOPE + C_ROPE),
                  ((0, 0), (0, 0), (0, LANES - C_NOPE - C_ROPE))).reshape(C_Q_RANK, N_HEADS * LANES)
    wukv = w_ukv[l].reshape(C_KV_RANK, N_HEADS, 2 * HEAD_DIM)
    wuk = jnp.pad(wukv[:, :, :C_NOPE], ((0, 0), (0, 0), (0, LANES - C_NOPE))).reshape(C_KV_RANK, N_HEADS * LANES)
    wuv = wukv[:, :, C_NOPE:].reshape(C_KV_RANK, BRANCH_W)
    bf_row = jnp.pad(b_f[l], (0, LANES - N_HEADS)).reshape(1, LANES)
    sink_row = jnp.repeat(sinks[l] * LOG2E, BLK).reshape(1, N_HEADS * BLK)
    return {
        "norm_g": norm_g[l].reshape(1, D_MODEL), "w1": w1, "wuq": wuq.astype(BF16), "wuk": wuk.astype(BF16),
        "wuv": wuv.astype(BF16), "q_norm_g": q_norm_g[l].reshape(1, C_Q_RANK),
        "kv_norm_g": kv_norm_g[l].reshape(1, C_KV_RANK), "b_f": bf_row, "sink": sink_row,
        "wz": z.astype(BF16), "wg": g.astype(BF16), "w_br": w_br[l].astype(BF16), "w_out": w_out[l].astype(BF16),
    }


def kernel(x, meta_tokens, norm_g, w_in, b_f, sinks, q_norm_g, kv_norm_g, w_uq, w_ukv, w_br, w_out, final_norm_g):
    batch, n_real, _ = x.shape
    seq = BLK + n_real
    depth = w_in.shape[0]
    assert (seq - BLK) % (Q_SUB * CHAINS_PER_GROUP) == 0 and K_TILE % Q_SUB == 0 and (batch * seq) % PRE_TM == 0 and (batch * seq) % POST_TM == 0
    h = jnp.concatenate([jnp.zeros((batch, PAD, D_MODEL), x.dtype),
                         jnp.broadcast_to(meta_tokens[None].astype(x.dtype), (batch, N_META, D_MODEL)),
                         x], axis=1).reshape(batch * seq, D_MODEL)
    eq, ek = _aug_matrices()
    dcap, mqcap = _dense_caps()
    consts = {"eq": eq, "ek": ek, "rope_tab": _rope_tables(seq, PRE_TM), "dcap": dcap, "mqcap": mqcap}
    swa_bias = _swa_biases()
    final_g = final_norm_g.reshape(1, D_MODEL)
    for l in range(depth):
        lw = _layer_weights(l, norm_g, w_in, b_f, sinks, q_norm_g, kv_norm_g, w_uq, w_ukv, w_br, w_out)
        qk, vt, qa, ka, vta = _pre_call(h, lw, consts, seq=seq)
        oa = _swa_call(qa, ka, vta, swa_bias, lw["sink"], batch=batch, seq=seq)
        obc = _dense_call(qk, vt, consts, batch=batch, seq=seq)
        h = _post_call(h, oa, obc, lw, final_g, final=(l == depth - 1))
    return h.reshape(batch, seq, D_MODEL)[:, BLK:]
```

```python
import functools
import math

import jax
import jax.numpy as jnp
import numpy as np
from jax import lax
from jax.experimental import pallas as pl
from jax.experimental.pallas import tpu as pltpu

D_MODEL = 1024
N_META = 16
BLK = 128
PAD = BLK - N_META
HEAD_DIM = 64
N_HEADS = 8
A_KV_HEADS = 2
C_Q_RANK = 256
C_KV_RANK = 128
C_NOPE = 64
C_ROPE = 32
ROPE_HALF = C_ROPE // 2
ROPE_THETA = 10000.0
N_BRANCH = 3
BRANCH_W = 512
EPS = 1e-6
NEG = -1e30
LOG2E = math.log2(math.e)

LANES = 128
Q_SUB = 512
K_TILE = 512
CHAINS_PER_GROUP = 1
LOOKAHEAD = 2
HEADS_PER_ITER = 2
MXU_CHUNKS = 1
SUM_ROWS = 16
PRE_TM = 512
POST_TM = 1024
VMEM_LIMIT_BYTES = 56 * 1024 * 1024

SCALE_AB = HEAD_DIM ** -0.5 * LOG2E
SCALE_C = (C_NOPE + C_ROPE) ** -0.5 * LOG2E

COL_AQ = 0
COL_AK = 512
COL_AV = 640
COL_BQ = 768
COL_BK = 1280
COL_BV = 1792
COL_CQ = 2304
COL_CKV = 2560
COL_MISC = 2688
N_PRE = 2816
MISC_ROPE = 64
AUG_ONE = 24
AUG_W = 6
AUG_K0 = N_HEADS * AUG_W

BF16 = jnp.bfloat16
F32 = jnp.float32


def _dot(a, b):
    return jnp.dot(a, b, preferred_element_type=F32)


def _dot_nt(a, b):
    return lax.dot_general(a, b, (((1,), (1,)), ((), ())), preferred_element_type=F32)


def _rms(x, g):
    return x * lax.rsqrt(jnp.mean(x * x, axis=-1, keepdims=True) + EPS) * g


def _split3(x):
    hi = x.astype(BF16).astype(F32)
    r = x - hi
    mid = r.astype(BF16).astype(F32)
    lo = (r - mid).astype(BF16).astype(F32)
    return hi, mid, lo


def _pre_kernel(x_ref, g_ref, w1_ref, wuq_ref, wuk_ref, wuv_ref, e_ref, tri_ref, qng_ref, kvng_ref,
                bf_ref, tab_ref, qk_ref, vt_ref, qa_ref, ka_ref, vta_ref, carry_ref, *, tm, seq):
    i = pl.program_id(0)
    hb = _rms(x_ref[...], g_ref[...]).astype(BF16)
    lane = lax.broadcasted_iota(jnp.int32, (tm, LANES), 1)
    row = lax.broadcasted_iota(jnp.int32, (tm, LANES), 0)
    lo_half = lane < HEAD_DIM

    misc = _dot(hb, w1_ref[:, COL_MISC:N_PRE])
    rc = _dot(hb, w1_ref[:, COL_CQ:COL_MISC])
    ra = _dot(hb, w1_ref[:, COL_AQ:COL_BQ])

    for j in range(N_HEADS // 2):
        pq = ra[:, LANES * j:LANES * (j + 1)] * SCALE_AB
        qa_ref[j] = jnp.where(lo_half, pq, 0.0).astype(BF16)
        qa_ref[j + N_HEADS // 2] = jnp.where(lo_half, 0.0, pq).astype(BF16)
    ka_ref[...] = ra[:, COL_AK:COL_AV].astype(BF16)
    vta_ref[...] = ra[:, COL_AV:COL_BQ].T.astype(BF16)

    p0 = lax.rem(i * tm, seq)
    pib = p0 + row
    pib = jnp.where(pib >= seq, pib - seq, pib)
    xg = misc + bf_ref[...]
    logf = (jnp.minimum(xg, 0.0) - jnp.log1p(jnp.exp(-jnp.abs(xg)))) * LOG2E
    logf = jnp.where((pib >= PAD) & (lane < N_HEADS), logf, 0.0)

    def pieces(v):
        rep = v + pltpu.roll(v, N_HEADS, 1) + pltpu.roll(v, 2 * N_HEADS, 1)
        hi, mid, lo = _split3(rep)
        return jnp.where(lane < N_HEADS, hi, jnp.where(lane < 2 * N_HEADS, mid, lo))

    def gather_pieces(v):
        return v + pltpu.roll(v, LANES - N_HEADS, 1) + pltpu.roll(v, LANES - 2 * N_HEADS, 1)

    bnd = seq - p0
    cum = gather_pieces(_dot(tri_ref[...], pieces(logf).astype(BF16)))
    before = jnp.sum(jnp.where(row < bnd, logf, 0.0), axis=0, keepdims=True)

    def rope(v, c, s1, s2):
        return v * c + pltpu.roll(v, LANES - ROPE_HALF, 1) * s1 + pltpu.roll(v, ROPE_HALF, 1) * s2

    cqn = _rms(rc[:, :C_Q_RANK], qng_ref[...]).astype(BF16)
    ckvn = _rms(rc[:, C_Q_RANK:], kvng_ref[...]).astype(BF16)
    qc = _dot(cqn, wuq_ref[...])
    for h in range(N_HEADS):
        blk = qc[:, LANES * h:LANES * (h + 1)]
        qk_ref[N_HEADS + h] = rope(blk, tab_ref[0], tab_ref[1], tab_ref[2]).astype(BF16)

    kc = _dot(ckvn, wuk_ref[...])
    kr = rope(misc, tab_ref[3], tab_ref[4], tab_ref[5])
    for h in range(N_HEADS):
        qk_ref[3 * N_HEADS + h] = jnp.where(lo_half, kc[:, LANES * h:LANES * (h + 1)], kr).astype(BF16)
    vt_ref[BRANCH_W:2 * BRANCH_W, :] = _dot(ckvn, wuv_ref[...]).T.astype(BF16)

    vt_ref[0:BRANCH_W, :] = _dot(hb, w1_ref[:, COL_BV:COL_CQ]).T.astype(BF16)

    @pl.when(i == 0)
    def _():
        carry_ref[...] = jnp.zeros_like(carry_ref)

    carry_in = jnp.where(p0 == 0, 0.0, carry_ref[0:1, :])
    cum = jnp.where(lane < N_HEADS, cum + jnp.where(row >= bnd, -before, carry_in), 0.0)
    carry_ref[...] = jnp.broadcast_to(cum[tm - 1:tm, :], carry_ref.shape)
    pc = jnp.where(lane == AUG_ONE, 1.0, pieces(cum)).astype(BF16)
    aug = _dot(pc, e_ref[...])

    rb = _dot(hb, w1_ref[:, COL_BQ:COL_BV])
    for h in range(N_HEADS):
        j, t = divmod(h, 2)
        own = lo_half if t == 0 else jnp.logical_not(lo_half)
        a = HEAD_DIM if t == 0 else 0
        in_aug = (lane >= a) & (lane < a + AUG_W)
        pq = rb[:, LANES * j:LANES * (j + 1)] * SCALE_AB
        pk = rb[:, BRANCH_W + LANES * j:BRANCH_W + LANES * (j + 1)]
        aq = pltpu.roll(aug, (a - AUG_W * h) % LANES, 1)
        ak = pltpu.roll(aug, (a - AUG_K0 - AUG_W * h) % LANES, 1)
        qk_ref[h] = jnp.where(own, pq, jnp.where(in_aug, aq, 0.0)).astype(BF16)
        qk_ref[2 * N_HEADS + h] = jnp.where(own, pk, jnp.where(in_aug, ak, 0.0)).astype(BF16)


def _const_spec(shape):
    nd = len(shape)
    return pl.BlockSpec(shape, lambda *_: (0,) * nd, pipeline_mode=pl.Buffered(1))


def _layer_spec(shape, l):
    nd = len(shape)
    return pl.BlockSpec((None,) + shape, lambda *_: (l,) + (0,) * nd, pipeline_mode=pl.Buffered(1))


def _pre_call(x, lw, l, consts, *, seq):
    m = x.shape[0]
    tm = PRE_TM
    n_tiles = m // tm
    period = seq // LANES
    grid_spec = pl.GridSpec(
        grid=(n_tiles,),
        in_specs=[
            pl.BlockSpec((tm, D_MODEL), lambda i: (i, 0)),
            _layer_spec((1, D_MODEL), l),
            _layer_spec((D_MODEL, N_PRE), l),
            _layer_spec((C_Q_RANK, N_HEADS * LANES), l),
            _layer_spec((C_KV_RANK, N_HEADS * LANES), l),
            _layer_spec((C_KV_RANK, BRANCH_W), l),
            _const_spec((LANES, LANES)),
            _const_spec((tm, tm)),
            _layer_spec((1, C_Q_RANK), l),
            _layer_spec((1, C_KV_RANK), l),
            _layer_spec((1, LANES), l),
            pl.BlockSpec((6, tm, LANES), lambda i: (0, i % period, 0)),
        ],
        out_specs=[
            pl.BlockSpec((4 * N_HEADS, tm, LANES), lambda i: (0, i, 0)),
            pl.BlockSpec((2 * BRANCH_W, tm), lambda i: (0, i)),
            pl.BlockSpec((N_HEADS, tm, LANES), lambda i: (0, i, 0)),
            pl.BlockSpec((tm, LANES), lambda i: (i, 0)),
            pl.BlockSpec((LANES, tm), lambda i: (0, i)),
        ],
        scratch_shapes=[pltpu.VMEM((8, LANES), F32)],
    )
    out_shape = [
        jax.ShapeDtypeStruct((4 * N_HEADS, m, LANES), BF16),
        jax.ShapeDtypeStruct((2 * BRANCH_W, m), BF16),
        jax.ShapeDtypeStruct((N_HEADS, m, LANES), BF16),
        jax.ShapeDtypeStruct((m, LANES), BF16),
        jax.ShapeDtypeStruct((LANES, m), BF16),
    ]
    return pl.pallas_call(
        functools.partial(_pre_kernel, tm=tm, seq=seq),
        grid_spec=grid_spec,
        out_shape=out_shape,
        compiler_params=pltpu.CompilerParams(
            dimension_semantics=("arbitrary",), vmem_limit_bytes=VMEM_LIMIT_BYTES),
        name="pre",
    )(x, lw["norm_g"], lw["w1"], lw["wuq"], lw["wuk"], lw["wuv"], consts["aug_e"], consts["tri"],
      lw["q_norm_g"], lw["kv_norm_g"], lw["b_f"], consts["rope_tab"])


def _swa_kernel(q_ref, k_ref, vt_ref, bias_ref, sink_ref, o_ref, ot_scr, *, seq):
    nb = seq // BLK
    half = N_HEADS // 2 * BLK
    sink = sink_ref[...]
    def scores(n):
        q = q_ref[:, n * BLK:(n + 1) * BLK, :].reshape(N_HEADS * BLK, LANES)
        return _dot_nt(k_ref[max(n - 1, 0) * BLK:(n + 1) * BLK, :], q)

    s_next = scores(0)
    for n in range(nb):
        k0 = max(n - 1, 0) * BLK
        k1 = (n + 1) * BLK
        if n == 0:
            bias = bias_ref[2, BLK:, :]
        else:
            bias = bias_ref[0 if n >= 2 else 1]
        s = jnp.minimum(s_next, -NEG) + bias
        if n + 1 < nb:
            s_next = scores(n + 1)
        m = jnp.maximum(jnp.max(s, axis=0, keepdims=True), sink)
        pb = jnp.exp2(s - m).astype(BF16)
        sink_p = jnp.exp2(sink - m)
        ones = jnp.ones((SUM_ROWS, k1 - k0), BF16)
        for kv in range(A_KV_HEADS):
            vt = jnp.concatenate([vt_ref[kv * HEAD_DIM:(kv + 1) * HEAD_DIM, k0:k1], ones], axis=0)
            pv = _dot(vt, pb[:, kv * half:(kv + 1) * half])
            pv = pv[:HEAD_DIM] / (pv[HEAD_DIM:HEAD_DIM + 1] + sink_p[:, kv * half:(kv + 1) * half])
            for g in range(N_HEADS // A_KV_HEADS):
                h = kv * (N_HEADS // A_KV_HEADS) + g
                ot_scr[h * HEAD_DIM:(h + 1) * HEAD_DIM, n * BLK:(n + 1) * BLK] = pv[:, g * BLK:(g + 1) * BLK]
    for c in range(nb):
        o_ref[c * BLK:(c + 1) * BLK, :] = ot_scr[:, c * BLK:(c + 1) * BLK].T.astype(o_ref.dtype)


def _swa_call(qa, ka, vta, bias, sink, l, *, batch, seq):
    m = ka.shape[0]
    grid_spec = pl.GridSpec(
        grid=(batch,),
        in_specs=[
            pl.BlockSpec((N_HEADS, seq, LANES), lambda b: (0, b, 0)),
            pl.BlockSpec((seq, LANES), lambda b: (b, 0)),
            pl.BlockSpec((LANES, seq), lambda b: (0, b)),
            _const_spec((3, 2 * BLK, N_HEADS * BLK)),
            _layer_spec((1, N_HEADS * BLK), l),
        ],
        out_specs=pl.BlockSpec((seq, BRANCH_W), lambda b: (b, 0)),
        scratch_shapes=[pltpu.VMEM((BRANCH_W, seq), F32)],
    )
    return pl.pallas_call(
        functools.partial(_swa_kernel, seq=seq),
        grid_spec=grid_spec,
        out_shape=jax.ShapeDtypeStruct((m, BRANCH_W), BF16),
        compiler_params=pltpu.CompilerParams(
            dimension_semantics=("parallel",), vmem_limit_bytes=VMEM_LIMIT_BYTES),
        name="swa",
    )(qa, ka, vta, bias, sink)


def _dense_kernel(q_ref, k_ref, vt_ref, dcap_ref, mqcap_ref, o_ref, ot_scr, *, seq):
    def head_chains(c):
        groups = [[(c, 0, BLK, [([(0, BLK)], [(mqcap_ref, 0, BLK)])])]]
        for u in range((seq - BLK) // Q_SUB):
            q0 = BLK + Q_SUB * u
            n_full = u * Q_SUB // K_TILE
            steps = [([(BLK + K_TILE * j, K_TILE)], None) for j in range(n_full)]
            rows = q0 + Q_SUB - (BLK + K_TILE * n_full)
            steps.append(([(0, BLK), (BLK + K_TILE * n_full, rows)],
                          [(dcap_ref, 0, BLK), (dcap_ref, BLK + K_TILE - rows, rows)]))
            if u % CHAINS_PER_GROUP == 0:
                groups.append([])
            groups[-1].append((c, q0, Q_SUB, steps))
        return groups

    def gather(ref_rows, ranges, axis):
        parts = [ref_rows(a, n) for a, n in ranges]
        return parts[0] if len(parts) == 1 else jnp.concatenate(parts, axis=axis)

    def split_ranges(ranges, parts):
        pieces = [(a + o, BLK) for a, n in ranges for o in range(0, n, BLK)]
        per = -(-len(pieces) // parts)
        out = []
        for i in range(0, len(pieces), per):
            merged = []
            for a, n in pieces[i:i + per]:
                if merged and merged[-1][0] + merged[-1][1] == a:
                    merged[-1] = (merged[-1][0], merged[-1][1] + n)
                else:
                    merged.append((a, n))
            out.append(merged)
        return out

    def head_body(hg, carry):
        heads = [hg * HEADS_PER_ITER + c for c in range(HEADS_PER_ITER)]
        rows = [pl.ds(pl.multiple_of(h * HEAD_DIM, HEAD_DIM), HEAD_DIM) for h in heads]
        per_head = [head_chains(c) for c in range(HEADS_PER_ITER)]
        slots = []
        for gi in range(len(per_head[0])):
            group = [ch for groups in per_head for ch in groups[gi]]
            for t in range(len(group[0][3])):
                slots += [(ch, t) for ch in group]

        def score_chunks(slot):
            (c, q0, qn, steps), t = slot
            q = q_ref[heads[c], q0:q0 + qn, :]
            return [functools.partial(
                lambda rg: _dot_nt(gather(lambda a, n: k_ref[heads[c], a:a + n, :], rg, 0), q), rg)
                for rg in split_ranges(steps[t][0], MXU_CHUNKS)]

        def cat(parts, axis):
            return parts[0] if len(parts) == 1 else jnp.concatenate(parts, axis=axis)

        ahead = [cat([f() for f in score_chunks(sl)], 0) for sl in slots[:LOOKAHEAD]]
        state = {}
        for n, ((c, q0, qn, steps), t) in enumerate(slots):
            s = ahead.pop(0)
            key_ranges, cap_ranges = steps[t]
            if cap_ranges is not None:
                cap = [ref[a:a + n, :] for ref, a, n in cap_ranges]
                s = jnp.minimum(s, cat(cap, 0))
            mc = jnp.max(s, axis=0, keepdims=True)
            prev = state.get((c, q0))
            m_new = mc if prev is None else jnp.maximum(prev[0], mc)
            pt = jnp.exp2(s - m_new).astype(BF16)
            vt = gather(lambda a, n: vt_ref[rows[c], a:a + n], key_ranges, 1)
            vt = jnp.concatenate([vt, jnp.ones((SUM_ROWS, vt.shape[1]), BF16)], axis=0)
            nxt = score_chunks(slots[n + LOOKAHEAD]) if n + LOOKAHEAD < len(slots) else []
            n_pv = MXU_CHUNKS if qn >= MXU_CHUNKS * LANES else 1
            w = qn // n_pv
            s_parts, pv_parts = [], []
            for j in range(max(len(nxt), n_pv)):
                if j < len(nxt):
                    s_parts.append(nxt[j]())
                if j < n_pv:
                    pv_parts.append(_dot(vt, pt[:, j * w:(j + 1) * w]))
            if nxt:
                ahead.append(cat(s_parts, 0))
            pv = cat(pv_parts, 1)
            acc = pv if prev is None else jnp.exp2(prev[0] - m_new) * prev[1] + pv
            state[c, q0] = (m_new, acc)
            if t == len(steps) - 1:
                del state[c, q0]
                ot_scr[rows[c], q0:q0 + qn] = acc[:HEAD_DIM] / acc[HEAD_DIM:HEAD_DIM + 1]
        return carry

    lax.fori_loop(0, N_HEADS // HEADS_PER_ITER, head_body, 0)
    for c in range(seq // BLK):
        o_ref[0, c * BLK:(c + 1) * BLK, :] = ot_scr[:, c * BLK:(c + 1) * BLK].T.astype(o_ref.dtype)


def _dense_call(qk, vt, consts, *, batch, seq):
    m = qk.shape[1]
    grid_spec = pl.GridSpec(
        grid=(batch, 2),
        in_specs=[
            pl.BlockSpec((N_HEADS, seq, LANES), lambda b, x: (x, b, 0)),
            pl.BlockSpec((N_HEADS, seq, LANES), lambda b, x: (2 + x, b, 0)),
            pl.BlockSpec((BRANCH_W, seq), lambda b, x: (x, b)),
            _const_spec((BLK + K_TILE, Q_SUB)),
            _const_spec((BLK, BLK)),
        ],
        out_specs=pl.BlockSpec((1, seq, BRANCH_W), lambda b, x: (x, b, 0)),
        scratch_shapes=[pltpu.VMEM((BRANCH_W, seq), F32)],
    )
    return pl.pallas_call(
        functools.partial(_dense_kernel, seq=seq),
        grid_spec=grid_spec,
        out_shape=jax.ShapeDtypeStruct((2, m, BRANCH_W), BF16),
        compiler_params=pltpu.CompilerParams(
            dimension_semantics=("parallel", "parallel"), vmem_limit_bytes=VMEM_LIMIT_BYTES),
        name="dense",
    )(qk, qk, vt, consts["dcap"], consts["mqcap"])


def _post_kernel(x_ref, oa_ref, obc_ref, g_ref, wz_ref, wg_ref, wbr_ref, wout_ref, fg_ref, out_ref, *, final):
    x = x_ref[...]
    hb = _rms(x, g_ref[...]).astype(BF16)
    y = None
    for i in range(N_BRANCH):
        o = (oa_ref[...] if i == 0 else obc_ref[i - 1]).astype(F32)
        z = _dot(hb, wz_ref[:, BRANCH_W * i:BRANCH_W * (i + 1)])
        gate = jax.nn.sigmoid(_dot(hb, wg_ref[:, D_MODEL * i:D_MODEL * (i + 1)]))
        u = (o * (z * jax.nn.sigmoid(z))).astype(BF16)
        t = _dot(u, wbr_ref[i])
        y = gate * t if y is None else y + gate * t
    out = x + _dot(y.astype(BF16), wout_ref[...])
    if final:
        out = _rms(out, fg_ref[...])
    out_ref[...] = out


def _post_call(x, oa, obc, lw, l, final_g, *, final):
    m = x.shape[0]
    tm = POST_TM
    grid_spec = pl.GridSpec(
        grid=(m // tm,),
        in_specs=[
            pl.BlockSpec((tm, D_MODEL), lambda i: (i, 0)),
            pl.BlockSpec((tm, BRANCH_W), lambda i: (i, 0)),
            pl.BlockSpec((2, tm, BRANCH_W), lambda i: (0, i, 0)),
            _layer_spec((1, D_MODEL), l),
            _layer_spec((D_MODEL, N_BRANCH * BRANCH_W), l),
            _layer_spec((D_MODEL, N_BRANCH * D_MODEL), l),
            _layer_spec((N_BRANCH, BRANCH_W, D_MODEL), l),
            _layer_spec((D_MODEL, D_MODEL), l),
            _const_spec((1, D_MODEL)),
        ],
        out_specs=pl.BlockSpec((tm, D_MODEL), lambda i: (i, 0)),
    )
    return pl.pallas_call(
        functools.partial(_post_kernel, final=final),
        grid_spec=grid_spec,
        out_shape=jax.ShapeDtypeStruct((m, D_MODEL), F32),
        input_output_aliases={0: 0},
        compiler_params=pltpu.CompilerParams(
            dimension_semantics=("parallel",), vmem_limit_bytes=VMEM_LIMIT_BYTES),
        name="post",
    )(x, oa, obc, lw["norm_g"], lw["wz"], lw["wg"], lw["w_br"], lw["w_out"], final_g)


def _rope_tables(seq, tm):
    pos = (jnp.arange(seq) - PAD).astype(F32)
    inv = ROPE_THETA ** (-jnp.arange(ROPE_HALF, dtype=F32) / ROPE_HALF)
    ang = pos[:, None] * inv[None, :]
    cos, sin = jnp.cos(ang), jnp.sin(ang)
    zero = jnp.zeros_like(cos)

    def lanes(nope, a, b):
        return jnp.concatenate([jnp.full((seq, C_NOPE), nope, F32), a, b,
                                jnp.zeros((seq, LANES - C_NOPE - C_ROPE), F32)], axis=1)

    tabs = jnp.stack([lanes(1.0, cos, cos) * SCALE_C, lanes(0.0, -sin, zero) * SCALE_C,
                      lanes(0.0, zero, sin) * SCALE_C,
                      lanes(0.0, cos, cos), lanes(0.0, -sin, zero), lanes(0.0, zero, sin)])
    return jnp.tile(tabs, (1, tm // LANES, 1))


def _aug_matrix():
    e = np.zeros((LANES, LANES), np.float32)
    for h in range(N_HEADS):
        q0, k0 = AUG_W * h, AUG_K0 + AUG_W * h
        for p in range(3):
            e[p * N_HEADS + h, q0 + p] = 1.0
            e[AUG_ONE, q0 + 3 + p] = 1.0
            e[AUG_ONE, k0 + p] = 1.0
            e[p * N_HEADS + h, k0 + 3 + p] = -1.0
    return jnp.asarray(e, BF16)


def _dense_caps():
    meta_ok = (np.arange(BLK)[:, None] >= PAD) & np.ones((1, Q_SUB), bool)
    kk = np.arange(K_TILE)[:, None] - (K_TILE - Q_SUB)
    diag_ok = kk <= np.arange(Q_SUB)[None, :]
    dcap = np.where(np.concatenate([meta_ok, diag_ok]), -NEG, NEG)
    km = np.arange(BLK)[:, None]
    mqcap = np.where((km >= PAD) & (km <= np.arange(BLK)[None, :]), -NEG, NEG)
    return jnp.asarray(dcap, F32), jnp.asarray(mqcap, F32)


def _swa_biases():
    slopes = 2.0 ** (-8.0 * (jnp.arange(N_HEADS, dtype=F32) + 1.0) / N_HEADS)
    s = np.arange(2 * BLK)[:, None]
    t = np.arange(BLK)[None, :]
    dist = BLK + t - s
    window = (dist >= 0) & (dist < BLK)
    tabs = []
    for mask in (window, window & (s >= PAD), window & (s >= BLK + PAD)):
        per_head = [jnp.where(mask, -slopes[h] * dist.astype(np.float32) * LOG2E, NEG) for h in range(N_HEADS)]
        tabs.append(jnp.concatenate(per_head, axis=1))
    return jnp.stack(tabs).astype(F32)


def _stacked_weights(norm_g, w_in, b_f, sinks, q_norm_g, kv_norm_g, w_uq, w_ukv, w_br, w_out):
    depth = w_in.shape[0]
    splits = np.cumsum([0, 512, 128, 128, 512, 512, 512, 8, 256, 128, 32, 1536, 3072])
    aq, ak, av, bq, bk, bv, bfl, cq, ckv, ckr, z, g = [w_in[:, :, splits[i]:splits[i + 1]] for i in range(12)]
    aq = aq.reshape(depth, D_MODEL, A_KV_HEADS, N_HEADS // A_KV_HEADS, HEAD_DIM)
    aq = jnp.transpose(aq, (0, 1, 3, 2, 4)).reshape(depth, D_MODEL, N_HEADS * HEAD_DIM)
    misc = jnp.concatenate([bfl, jnp.zeros((depth, D_MODEL, MISC_ROPE - N_HEADS), F32), ckr,
                            jnp.zeros((depth, D_MODEL, LANES - MISC_ROPE - C_ROPE), F32)], axis=2)
    w1 = jnp.concatenate([aq, ak, av, bq, bk, bv, cq, ckv, misc], axis=2).astype(BF16)
    wuq = jnp.pad(w_uq.reshape(depth, C_Q_RANK, N_HEADS, C_NOPE + C_ROPE),
                  ((0, 0), (0, 0), (0, 0), (0, LANES - C_NOPE - C_ROPE))).reshape(depth, C_Q_RANK, N_HEADS * LANES)
    wukv = w_ukv.reshape(depth, C_KV_RANK, N_HEADS, 2 * HEAD_DIM)
    wuk = jnp.pad(wukv[..., :C_NOPE], ((0, 0), (0, 0), (0, 0), (0, LANES - C_NOPE)))
    wuk = wuk.reshape(depth, C_KV_RANK, N_HEADS * LANES)
    wuv = wukv[..., C_NOPE:].reshape(depth, C_KV_RANK, BRANCH_W)
    return {
        "norm_g": norm_g.reshape(depth, 1, D_MODEL), "w1": w1, "wuq": wuq.astype(BF16), "wuk": wuk.astype(BF16),
        "wuv": wuv.astype(BF16), "q_norm_g": q_norm_g.reshape(depth, 1, C_Q_RANK),
        "kv_norm_g": kv_norm_g.reshape(depth, 1, C_KV_RANK),
        "b_f": jnp.pad(b_f, ((0, 0), (0, LANES - N_HEADS))).reshape(depth, 1, LANES),
        "sink": jnp.repeat(sinks * LOG2E, BLK, axis=1).reshape(depth, 1, N_HEADS * BLK),
        "wz": z.astype(BF16), "wg": g.astype(BF16), "w_br": w_br.astype(BF16), "w_out": w_out.astype(BF16),
    }


def kernel(x, meta_tokens, norm_g, w_in, b_f, sinks, q_norm_g, kv_norm_g, w_uq, w_ukv, w_br, w_out, final_norm_g):
    batch, n_real, _ = x.shape
    seq = BLK + n_real
    depth = w_in.shape[0]
    assert (seq - BLK) % (Q_SUB * CHAINS_PER_GROUP) == 0 and K_TILE % Q_SUB == 0 and (batch * seq) % PRE_TM == 0 and (batch * seq) % POST_TM == 0
    h = jnp.concatenate([jnp.zeros((batch, PAD, D_MODEL), x.dtype),
                         jnp.broadcast_to(meta_tokens[None].astype(x.dtype), (batch, N_META, D_MODEL)),
                         x], axis=1).reshape(batch * seq, D_MODEL)
    dcap, mqcap = _dense_caps()
    consts = {"aug_e": _aug_matrix(), "tri": jnp.asarray(np.tri(PRE_TM), BF16),
              "rope_tab": _rope_tables(seq, PRE_TM), "dcap": dcap, "mqcap": mqcap}
    swa_bias = _swa_biases()
    final_g = final_norm_g.reshape(1, D_MODEL)
    lw = _stacked_weights(norm_g, w_in, b_f, sinks, q_norm_g, kv_norm_g, w_uq, w_ukv, w_br, w_out)
    for l in range(depth):
        qk, vt, qa, ka, vta = _pre_call(h, lw, l, consts, seq=seq)
        oa = _swa_call(qa, ka, vta, swa_bias, lw["sink"], l, batch=batch, seq=seq)
        obc = _dense_call(qk, vt, consts, batch=batch, seq=seq)
        h = _post_call(h, oa, obc, lw, l, final_g, final=(l == depth - 1))
    return h.reshape(batch, seq, D_MODEL)[:, BLK:]
```

```python
import functools
import math

import jax
import jax.numpy as jnp
import numpy as np
from jax import lax
from jax.experimental import pallas as pl
from jax.experimental.pallas import tpu as pltpu

D_MODEL = 1024
N_META = 16
BLK = 128
PAD = BLK - N_META
HEAD_DIM = 64
N_HEADS = 8
A_KV_HEADS = 2
C_Q_RANK = 256
C_KV_RANK = 128
C_NOPE = 64
C_ROPE = 32
ROPE_HALF = C_ROPE // 2
ROPE_THETA = 10000.0
N_BRANCH = 3
BRANCH_W = 512
EPS = 1e-6
NEG = -1e30
LOG2E = math.log2(math.e)

LANES = 128
Q_SUB = 512
K_TILE = 512
CHAINS_PER_GROUP = 1
LOOKAHEAD = 1
HEADS_PER_ITER = 1
MXU_CHUNKS = 1
SUM_ROWS = 16
PRE_TM = 512
POST_TM = 1024
VMEM_LIMIT_BYTES = 56 * 1024 * 1024

SCALE_AB = HEAD_DIM ** -0.5 * LOG2E
SCALE_C = (C_NOPE + C_ROPE) ** -0.5 * LOG2E

COL_AQ = 0
COL_AK = 512
COL_AV = 640
COL_BQ = 768
COL_BK = 1280
COL_BV = 1792
COL_CQ = 2304
COL_CKV = 2560
COL_MISC = 2688
N_PRE = 2816
MISC_ROPE = 64
AUG_ONE = 24
AUG_W = 6
AUG_K0 = N_HEADS * AUG_W

BF16 = jnp.bfloat16
F32 = jnp.float32


def _dot(a, b):
    return jnp.dot(a, b, preferred_element_type=F32)


def _dot_nt(a, b):
    return lax.dot_general(a, b, (((1,), (1,)), ((), ())), preferred_element_type=F32)


def _rms(x, g):
    return x * lax.rsqrt(jnp.mean(x * x, axis=-1, keepdims=True) + EPS) * g


def _split3(x):
    hi = x.astype(BF16).astype(F32)
    r = x - hi
    mid = r.astype(BF16).astype(F32)
    lo = (r - mid).astype(BF16).astype(F32)
    return hi, mid, lo


def _pre_kernel(x_ref, g_ref, w1_ref, wuq_ref, wuk_ref, wuv_ref, e_ref, tri_ref, qng_ref, kvng_ref,
                bf_ref, tab_ref, qk_ref, vt_ref, qa_ref, ka_ref, vta_ref, carry_ref, *, tm, seq):
    i = pl.program_id(0)
    hb = _rms(x_ref[...], g_ref[...]).astype(BF16)
    lane = lax.broadcasted_iota(jnp.int32, (tm, LANES), 1)
    row = lax.broadcasted_iota(jnp.int32, (tm, LANES), 0)
    lo_half = lane < HEAD_DIM

    misc = _dot(hb, w1_ref[:, COL_MISC:N_PRE])
    rc = _dot(hb, w1_ref[:, COL_CQ:COL_MISC])
    ra = _dot(hb, w1_ref[:, COL_AQ:COL_BQ])

    for j in range(N_HEADS // 2):
        pq = ra[:, LANES * j:LANES * (j + 1)] * SCALE_AB
        qa_ref[j] = jnp.where(lo_half, pq, 0.0).astype(BF16)
        qa_ref[j + N_HEADS // 2] = jnp.where(lo_half, 0.0, pq).astype(BF16)
    ka_ref[...] = ra[:, COL_AK:COL_AV].astype(BF16)
    vta_ref[...] = ra[:, COL_AV:COL_BQ].T.astype(BF16)

    p0 = lax.rem(i * tm, seq)
    pib = p0 + row
    pib = jnp.where(pib >= seq, pib - seq, pib)
    xg = misc + bf_ref[...]
    logf = (jnp.minimum(xg, 0.0) - jnp.log1p(jnp.exp(-jnp.abs(xg)))) * LOG2E
    logf = jnp.where((pib >= PAD) & (lane < N_HEADS), logf, 0.0)

    def pieces(v):
        rep = v + pltpu.roll(v, N_HEADS, 1) + pltpu.roll(v, 2 * N_HEADS, 1)
        hi, mid, lo = _split3(rep)
        return jnp.where(lane < N_HEADS, hi, jnp.where(lane < 2 * N_HEADS, mid, lo))

    def gather_pieces(v):
        return v + pltpu.roll(v, LANES - N_HEADS, 1) + pltpu.roll(v, LANES - 2 * N_HEADS, 1)

    bnd = seq - p0
    cum = gather_pieces(_dot(tri_ref[...], pieces(logf).astype(BF16)))
    before = jnp.sum(jnp.where(row < bnd, logf, 0.0), axis=0, keepdims=True)

    def rope(v, c, s1, s2):
        return v * c + pltpu.roll(v, LANES - ROPE_HALF, 1) * s1 + pltpu.roll(v, ROPE_HALF, 1) * s2

    cqn = _rms(rc[:, :C_Q_RANK], qng_ref[...]).astype(BF16)
    ckvn = _rms(rc[:, C_Q_RANK:], kvng_ref[...]).astype(BF16)
    qc = _dot(cqn, wuq_ref[...])
    for h in range(N_HEADS):
        blk = qc[:, LANES * h:LANES * (h + 1)]
        qk_ref[N_HEADS + h] = rope(blk, tab_ref[0], tab_ref[1], tab_ref[2]).astype(BF16)

    kc = _dot(ckvn, wuk_ref[...])
    kr = rope(misc, tab_ref[3], tab_ref[4], tab_ref[5])
    for h in range(N_HEADS):
        qk_ref[3 * N_HEADS + h] = jnp.where(lo_half, kc[:, LANES * h:LANES * (h + 1)], kr).astype(BF16)
    vt_ref[BRANCH_W:2 * BRANCH_W, :] = _dot(ckvn, wuv_ref[...]).T.astype(BF16)

    vt_ref[0:BRANCH_W, :] = _dot(hb, w1_ref[:, COL_BV:COL_CQ]).T.astype(BF16)

    @pl.when(i == 0)
    def _():
        carry_ref[...] = jnp.zeros_like(carry_ref)

    carry_in = jnp.where(p0 == 0, 0.0, carry_ref[0:1, :])
    cum = jnp.where(lane < N_HEADS, cum + jnp.where(row >= bnd, -before, carry_in), 0.0)
    carry_ref[...] = jnp.broadcast_to(cum[tm - 1:tm, :], carry_ref.shape)
    pc = jnp.where(lane == AUG_ONE, 1.0, pieces(cum)).astype(BF16)
    aug = _dot(pc, e_ref[...])

    rb = _dot(hb, w1_ref[:, COL_BQ:COL_BV])
    for h in range(N_HEADS):
        j, t = divmod(h, 2)
        own = lo_half if t == 0 else jnp.logical_not(lo_half)
        a = HEAD_DIM if t == 0 else 0
        in_aug = (lane >= a) & (lane < a + AUG_W)
        pq = rb[:, LANES * j:LANES * (j + 1)] * SCALE_AB
        pk = rb[:, BRANCH_W + LANES * j:BRANCH_W + LANES * (j + 1)]
        aq = pltpu.roll(aug, (a - AUG_W * h) % LANES, 1)
        ak = pltpu.roll(aug, (a - AUG_K0 - AUG_W * h) % LANES, 1)
        qk_ref[h] = jnp.where(own, pq, jnp.where(in_aug, aq, 0.0)).astype(BF16)
        qk_ref[2 * N_HEADS + h] = jnp.where(own, pk, jnp.where(in_aug, ak, 0.0)).astype(BF16)


def _const_spec(shape):
    nd = len(shape)
    return pl.BlockSpec(shape, lambda *_: (0,) * nd, pipeline_mode=pl.Buffered(1))


def _layer_spec(shape, l):
    nd = len(shape)
    return pl.BlockSpec((None,) + shape, lambda *_: (l,) + (0,) * nd, pipeline_mode=pl.Buffered(1))


def _pre_call(x, lw, l, consts, *, seq):
    m = x.shape[0]
    tm = PRE_TM
    n_tiles = m // tm
    period = seq // LANES
    grid_spec = pl.GridSpec(
        grid=(n_tiles,),
        in_specs=[
            pl.BlockSpec((tm, D_MODEL), lambda i: (i, 0)),
            _layer_spec((1, D_MODEL), l),
            _layer_spec((D_MODEL, N_PRE), l),
            _layer_spec((C_Q_RANK, N_HEADS * LANES), l),
            _layer_spec((C_KV_RANK, N_HEADS * LANES), l),
            _layer_spec((C_KV_RANK, BRANCH_W), l),
            _const_spec((LANES, LANES)),
            _const_spec((tm, tm)),
            _layer_spec((1, C_Q_RANK), l),
            _layer_spec((1, C_KV_RANK), l),
            _layer_spec((1, LANES), l),
            pl.BlockSpec((6, tm, LANES), lambda i: (0, i % period, 0)),
        ],
        out_specs=[
            pl.BlockSpec((4 * N_HEADS, tm, LANES), lambda i: (0, i, 0)),
            pl.BlockSpec((2 * BRANCH_W, tm), lambda i: (0, i)),
            pl.BlockSpec((N_HEADS, tm, LANES), lambda i: (0, i, 0)),
            pl.BlockSpec((tm, LANES), lambda i: (i, 0)),
            pl.BlockSpec((LANES, tm), lambda i: (0, i)),
        ],
        scratch_shapes=[pltpu.VMEM((8, LANES), F32)],
    )
    out_shape = [
        jax.ShapeDtypeStruct((4 * N_HEADS, m, LANES), BF16),
        jax.ShapeDtypeStruct((2 * BRANCH_W, m), BF16),
        jax.ShapeDtypeStruct((N_HEADS, m, LANES), BF16),
        jax.ShapeDtypeStruct((m, LANES), BF16),
        jax.ShapeDtypeStruct((LANES, m), BF16),
    ]
    return pl.pallas_call(
        functools.partial(_pre_kernel, tm=tm, seq=seq),
        grid_spec=grid_spec,
        out_shape=out_shape,
        compiler_params=pltpu.CompilerParams(
            dimension_semantics=("arbitrary",), vmem_limit_bytes=VMEM_LIMIT_BYTES),
        name="pre",
    )(x, lw["norm_g"], lw["w1"], lw["wuq"], lw["wuk"], lw["wuv"], consts["aug_e"], consts["tri"],
      lw["q_norm_g"], lw["kv_norm_g"], lw["b_f"], consts["rope_tab"])


def _swa_kernel(q_ref, k_ref, vt_ref, bias_ref, sink_ref, o_ref, ot_scr, *, seq):
    nb = seq // BLK
    half = N_HEADS // 2 * BLK
    sink = sink_ref[...]
    def scores(n):
        q = q_ref[:, n * BLK:(n + 1) * BLK, :].reshape(N_HEADS * BLK, LANES)
        return _dot_nt(k_ref[max(n - 1, 0) * BLK:(n + 1) * BLK, :], q)

    s_next = scores(0)
    for n in range(nb):
        k0 = max(n - 1, 0) * BLK
        k1 = (n + 1) * BLK
        if n == 0:
            bias = bias_ref[2, BLK:, :]
        else:
            bias = bias_ref[0 if n >= 2 else 1]
        s = jnp.minimum(s_next, -NEG) + bias
        if n + 1 < nb:
            s_next = scores(n + 1)
        m = jnp.maximum(jnp.max(s, axis=0, keepdims=True), sink)
        pb = jnp.exp2(s - m).astype(BF16)
        sink_p = jnp.exp2(sink - m)
        ones = jnp.ones((SUM_ROWS, k1 - k0), BF16)
        for kv in range(A_KV_HEADS):
            vt = jnp.concatenate([vt_ref[kv * HEAD_DIM:(kv + 1) * HEAD_DIM, k0:k1], ones], axis=0)
            pv = _dot(vt, pb[:, kv * half:(kv + 1) * half])
            pv = pv[:HEAD_DIM] / (pv[HEAD_DIM:HEAD_DIM + 1] + sink_p[:, kv * half:(kv + 1) * half])
            for g in range(N_HEADS // A_KV_HEADS):
                h = kv * (N_HEADS // A_KV_HEADS) + g
                ot_scr[h * HEAD_DIM:(h + 1) * HEAD_DIM, n * BLK:(n + 1) * BLK] = pv[:, g * BLK:(g + 1) * BLK]
    for c in range(nb):
        o_ref[c * BLK:(c + 1) * BLK, :] = ot_scr[:, c * BLK:(c + 1) * BLK].T.astype(o_ref.dtype)


def _swa_call(qa, ka, vta, bias, sink, l, *, batch, seq):
    m = ka.shape[0]
    grid_spec = pl.GridSpec(
        grid=(batch,),
        in_specs=[
            pl.BlockSpec((N_HEADS, seq, LANES), lambda b: (0, b, 0)),
            pl.BlockSpec((seq, LANES), lambda b: (b, 0)),
            pl.BlockSpec((LANES, seq), lambda b: (0, b)),
            _const_spec((3, 2 * BLK, N_HEADS * BLK)),
            _layer_spec((1, N_HEADS * BLK), l),
        ],
        out_specs=pl.BlockSpec((seq, BRANCH_W), lambda b: (b, 0)),
        scratch_shapes=[pltpu.VMEM((BRANCH_W, seq), F32)],
    )
    return pl.pallas_call(
        functools.partial(_swa_kernel, seq=seq),
        grid_spec=grid_spec,
        out_shape=jax.ShapeDtypeStruct((m, BRANCH_W), BF16),
        compiler_params=pltpu.CompilerParams(
            dimension_semantics=("parallel",), vmem_limit_bytes=VMEM_LIMIT_BYTES),
        name="swa",
    )(qa, ka, vta, bias, sink)


def _dense_kernel(q_ref, k_ref, vt_ref, dcap_ref, mqcap_ref, o_ref, ot_scr, *, seq):
    def head_chains(c):
        groups = [[(c, 0, BLK, [([(0, BLK)], [(mqcap_ref, 0, BLK)])])]]
        for u in range((seq - BLK) // Q_SUB):
            q0 = BLK + Q_SUB * u
            n_full = u * Q_SUB // K_TILE
            steps = [([(BLK + K_TILE * j, K_TILE)], None) for j in range(n_full)]
            rows = q0 + Q_SUB - (BLK + K_TILE * n_full)
            steps.append(([(0, BLK), (BLK + K_TILE * n_full, rows)],
                          [(dcap_ref, 0, BLK), (dcap_ref, BLK + K_TILE - rows, rows)]))
            if u % CHAINS_PER_GROUP == 0:
                groups.append([])
            groups[-1].append((c, q0, Q_SUB, steps))
        return groups

    def gather(ref_rows, ranges, axis):
        parts = [ref_rows(a, n) for a, n in ranges]
        return parts[0] if len(parts) == 1 else jnp.concatenate(parts, axis=axis)

    def split_ranges(ranges, parts):
        pieces = [(a + o, BLK) for a, n in ranges for o in range(0, n, BLK)]
        per = -(-len(pieces) // parts)
        out = []
        for i in range(0, len(pieces), per):
            merged = []
            for a, n in pieces[i:i + per]:
                if merged and merged[-1][0] + merged[-1][1] == a:
                    merged[-1] = (merged[-1][0], merged[-1][1] + n)
                else:
                    merged.append((a, n))
            out.append(merged)
        return out

    def head_body(hg, carry):
        heads = [hg * HEADS_PER_ITER + c for c in range(HEADS_PER_ITER)]
        rows = [pl.ds(pl.multiple_of(h * HEAD_DIM, HEAD_DIM), HEAD_DIM) for h in heads]
        per_head = [head_chains(c) for c in range(HEADS_PER_ITER)]
        slots = []
        for gi in range(len(per_head[0])):
            group = [ch for groups in per_head for ch in groups[gi]]
            for t in range(len(group[0][3])):
                slots += [(ch, t) for ch in group]

        def score_chunks(slot):
            (c, q0, qn, steps), t = slot
            q = q_ref[heads[c], q0:q0 + qn, :]
            return [functools.partial(
                lambda rg: _dot_nt(gather(lambda a, n: k_ref[heads[c], a:a + n, :], rg, 0), q), rg)
                for rg in split_ranges(steps[t][0], MXU_CHUNKS)]

        def cat(parts, axis):
            return parts[0] if len(parts) == 1 else jnp.concatenate(parts, axis=axis)

        ahead = [cat([f() for f in score_chunks(sl)], 0) for sl in slots[:LOOKAHEAD]]
        state = {}
        for n, ((c, q0, qn, steps), t) in enumerate(slots):
            s = ahead.pop(0)
            key_ranges, cap_ranges = steps[t]
            if cap_ranges is not None:
                cap = [ref[a:a + n, :] for ref, a, n in cap_ranges]
                s = jnp.minimum(s, cat(cap, 0))
            mc = jnp.max(s, axis=0, keepdims=True)
            prev = state.get((c, q0))
            m_new = mc if prev is None else jnp.maximum(prev[0], mc)
            pt = jnp.exp2(s - m_new).astype(BF16)
            vt = gather(lambda a, n: vt_ref[rows[c], a:a + n], key_ranges, 1)
            vt = jnp.concatenate([vt, jnp.ones((SUM_ROWS, vt.shape[1]), BF16)], axis=0)
            nxt = score_chunks(slots[n + LOOKAHEAD]) if n + LOOKAHEAD < len(slots) else []
            n_pv = MXU_CHUNKS if qn >= MXU_CHUNKS * LANES else 1
            w = qn // n_pv
            s_parts, pv_parts = [], []
            for j in range(max(len(nxt), n_pv)):
                if j < len(nxt):
                    s_parts.append(nxt[j]())
                if j < n_pv:
                    pv_parts.append(_dot(vt, pt[:, j * w:(j + 1) * w]))
            if nxt:
                ahead.append(cat(s_parts, 0))
            pv = cat(pv_parts, 1)
            acc = pv if prev is None else jnp.exp2(prev[0] - m_new) * prev[1] + pv
            state[c, q0] = (m_new, acc)
            if t == len(steps) - 1:
                del state[c, q0]
                ot_scr[rows[c], q0:q0 + qn] = acc[:HEAD_DIM] / acc[HEAD_DIM:HEAD_DIM + 1]
        return carry

    lax.fori_loop(0, N_HEADS // HEADS_PER_ITER, head_body, 0)
    for c in range(seq // BLK):
        o_ref[0, c * BLK:(c + 1) * BLK, :] = ot_scr[:, c * BLK:(c + 1) * BLK].T.astype(o_ref.dtype)


def _dense_call(qk, vt, consts, *, batch, seq):
    m = qk.shape[1]
    grid_spec = pl.GridSpec(
        grid=(batch, 2),
        in_specs=[
            pl.BlockSpec((N_HEADS, seq, LANES), lambda b, x: (x, b, 0)),
            pl.BlockSpec((N_HEADS, seq, LANES), lambda b, x: (2 + x, b, 0)),
            pl.BlockSpec((BRANCH_W, seq), lambda b, x: (x, b)),
            _const_spec((BLK + K_TILE, Q_SUB)),
            _const_spec((BLK, BLK)),
        ],
        out_specs=pl.BlockSpec((1, seq, BRANCH_W), lambda b, x: (x, b, 0)),
        scratch_shapes=[pltpu.VMEM((BRANCH_W, seq), F32)],
    )
    return pl.pallas_call(
        functools.partial(_dense_kernel, seq=seq),
        grid_spec=grid_spec,
        out_shape=jax.ShapeDtypeStruct((2, m, BRANCH_W), BF16),
        compiler_params=pltpu.CompilerParams(
            dimension_semantics=("parallel", "parallel"), vmem_limit_bytes=VMEM_LIMIT_BYTES),
        name="dense",
    )(qk, qk, vt, consts["dcap"], consts["mqcap"])


def _post_kernel(x_ref, oa_ref, obc_ref, g_ref, wz_ref, wg_ref, wbr_ref, wout_ref, fg_ref, out_ref, *, final):
    x = x_ref[...]
    hb = _rms(x, g_ref[...]).astype(BF16)
    y = None
    for i in range(N_BRANCH):
        o = (oa_ref[...] if i == 0 else obc_ref[i - 1]).astype(F32)
        z = _dot(hb, wz_ref[:, BRANCH_W * i:BRANCH_W * (i + 1)])
        gate = jax.nn.sigmoid(_dot(hb, wg_ref[:, D_MODEL * i:D_MODEL * (i + 1)]))
        u = (o * (z * jax.nn.sigmoid(z))).astype(BF16)
        t = _dot(u, wbr_ref[i])
        y = gate * t if y is None else y + gate * t
    out = x + _dot(y.astype(BF16), wout_ref[...])
    if final:
        out = _rms(out, fg_ref[...])
    out_ref[...] = out


def _post_call(x, oa, obc, lw, l, final_g, *, final, batch, seq):
    m = x.shape[0]
    tm = POST_TM
    if final:
        n_t = (seq - BLK) // tm
        grid = (batch, n_t)
        m_out = batch * (seq - BLK)

        def row0(b, t):
            return pl.multiple_of(b * seq + BLK + t * tm, BLK)

        el = pl.Element
        stream_specs = [
            pl.BlockSpec((el(tm), el(D_MODEL)), lambda b, t: (row0(b, t), 0)),
            pl.BlockSpec((el(tm), el(BRANCH_W)), lambda b, t: (row0(b, t), 0)),
            pl.BlockSpec((el(2), el(tm), el(BRANCH_W)), lambda b, t: (0, row0(b, t), 0)),
        ]
        out_spec = pl.BlockSpec((tm, D_MODEL), lambda b, t: (b * n_t + t, 0))
    else:
        grid = (m // tm,)
        m_out = m
        stream_specs = [
            pl.BlockSpec((tm, D_MODEL), lambda i: (i, 0)),
            pl.BlockSpec((tm, BRANCH_W), lambda i: (i, 0)),
            pl.BlockSpec((2, tm, BRANCH_W), lambda i: (0, i, 0)),
        ]
        out_spec = pl.BlockSpec((tm, D_MODEL), lambda i: (i, 0))
    grid_spec = pl.GridSpec(
        grid=grid,
        in_specs=stream_specs + [
            _layer_spec((1, D_MODEL), l),
            _layer_spec((D_MODEL, N_BRANCH * BRANCH_W), l),
            _layer_spec((D_MODEL, N_BRANCH * D_MODEL), l),
            _layer_spec((N_BRANCH, BRANCH_W, D_MODEL), l),
            _layer_spec((D_MODEL, D_MODEL), l),
            _const_spec((1, D_MODEL)),
        ],
        out_specs=out_spec,
    )
    return pl.pallas_call(
        functools.partial(_post_kernel, final=final),
        grid_spec=grid_spec,
        out_shape=jax.ShapeDtypeStruct((m_out, D_MODEL), F32),
        input_output_aliases={} if final else {0: 0},
        compiler_params=pltpu.CompilerParams(
            dimension_semantics=("parallel",) * len(grid), vmem_limit_bytes=VMEM_LIMIT_BYTES),
        name="post",
    )(x, oa, obc, lw["norm_g"], lw["wz"], lw["wg"], lw["w_br"], lw["w_out"], final_g)


def _rope_tables(seq, tm):
    pos = (jnp.arange(seq) - PAD).astype(F32)
    inv = ROPE_THETA ** (-jnp.arange(ROPE_HALF, dtype=F32) / ROPE_HALF)
    ang = pos[:, None] * inv[None, :]
    cos, sin = jnp.cos(ang), jnp.sin(ang)
    zero = jnp.zeros_like(cos)

    def lanes(nope, a, b):
        return jnp.concatenate([jnp.full((seq, C_NOPE), nope, F32), a, b,
                                jnp.zeros((seq, LANES - C_NOPE - C_ROPE), F32)], axis=1)

    tabs = jnp.stack([lanes(1.0, cos, cos) * SCALE_C, lanes(0.0, -sin, zero) * SCALE_C,
                      lanes(0.0, zero, sin) * SCALE_C,
                      lanes(0.0, cos, cos), lanes(0.0, -sin, zero), lanes(0.0, zero, sin)])
    return jnp.tile(tabs, (1, tm // LANES, 1))


def _aug_matrix():
    e = np.zeros((LANES, LANES), np.float32)
    for h in range(N_HEADS):
        q0, k0 = AUG_W * h, AUG_K0 + AUG_W * h
        for p in range(3):
            e[p * N_HEADS + h, q0 + p] = 1.0
            e[AUG_ONE, q0 + 3 + p] = 1.0
            e[AUG_ONE, k0 + p] = 1.0
            e[p * N_HEADS + h, k0 + 3 + p] = -1.0
    return jnp.asarray(e, BF16)


def _dense_caps():
    meta_ok = (np.arange(BLK)[:, None] >= PAD) & np.ones((1, Q_SUB), bool)
    kk = np.arange(K_TILE)[:, None] - (K_TILE - Q_SUB)
    diag_ok = kk <= np.arange(Q_SUB)[None, :]
    dcap = np.where(np.concatenate([meta_ok, diag_ok]), -NEG, NEG)
    km = np.arange(BLK)[:, None]
    mqcap = np.where((km >= PAD) & (km <= np.arange(BLK)[None, :]), -NEG, NEG)
    return jnp.asarray(dcap, F32), jnp.asarray(mqcap, F32)


def _swa_biases():
    slopes = 2.0 ** (-8.0 * (jnp.arange(N_HEADS, dtype=F32) + 1.0) / N_HEADS)
    s = np.arange(2 * BLK)[:, None]
    t = np.arange(BLK)[None, :]
    dist = BLK + t - s
    window = (dist >= 0) & (dist < BLK)
    tabs = []
    for mask in (window, window & (s >= PAD), window & (s >= BLK + PAD)):
        per_head = [jnp.where(mask, -slopes[h] * dist.astype(np.float32) * LOG2E, NEG) for h in range(N_HEADS)]
        tabs.append(jnp.concatenate(per_head, axis=1))
    return jnp.stack(tabs).astype(F32)


def _stacked_weights(norm_g, w_in, b_f, sinks, q_norm_g, kv_norm_g, w_uq, w_ukv, w_br, w_out):
    depth = w_in.shape[0]
    splits = np.cumsum([0, 512, 128, 128, 512, 512, 512, 8, 256, 128, 32, 1536, 3072])
    aq, ak, av, bq, bk, bv, bfl, cq, ckv, ckr, z, g = [w_in[:, :, splits[i]:splits[i + 1]] for i in range(12)]
    aq = aq.reshape(depth, D_MODEL, A_KV_HEADS, N_HEADS // A_KV_HEADS, HEAD_DIM)
    aq = jnp.transpose(aq, (0, 1, 3, 2, 4)).reshape(depth, D_MODEL, N_HEADS * HEAD_DIM)
    misc = jnp.concatenate([bfl, jnp.zeros((depth, D_MODEL, MISC_ROPE - N_HEADS), F32), ckr,
                            jnp.zeros((depth, D_MODEL, LANES - MISC_ROPE - C_ROPE), F32)], axis=2)
    w1 = jnp.concatenate([aq, ak, av, bq, bk, bv, cq, ckv, misc], axis=2).astype(BF16)
    wuq = jnp.pad(w_uq.reshape(depth, C_Q_RANK, N_HEADS, C_NOPE + C_ROPE),
                  ((0, 0), (0, 0), (0, 0), (0, LANES - C_NOPE - C_ROPE))).reshape(depth, C_Q_RANK, N_HEADS * LANES)
    wukv = w_ukv.reshape(depth, C_KV_RANK, N_HEADS, 2 * HEAD_DIM)
    wuk = jnp.pad(wukv[..., :C_NOPE], ((0, 0), (0, 0), (0, 0), (0, LANES - C_NOPE)))
    wuk = wuk.reshape(depth, C_KV_RANK, N_HEADS * LANES)
    wuv = wukv[..., C_NOPE:].reshape(depth, C_KV_RANK, BRANCH_W)
    return {
        "norm_g": norm_g.reshape(depth, 1, D_MODEL), "w1": w1, "wuq": wuq.astype(BF16), "wuk": wuk.astype(BF16),
        "wuv": wuv.astype(BF16), "q_norm_g": q_norm_g.reshape(depth, 1, C_Q_RANK),
        "kv_norm_g": kv_norm_g.reshape(depth, 1, C_KV_RANK),
        "b_f": jnp.pad(b_f, ((0, 0), (0, LANES - N_HEADS))).reshape(depth, 1, LANES),
        "sink": jnp.repeat(sinks * LOG2E, BLK, axis=1).reshape(depth, 1, N_HEADS * BLK),
        "wz": z.astype(BF16), "wg": g.astype(BF16), "w_br": w_br.astype(BF16), "w_out": w_out.astype(BF16),
    }


def kernel(x, meta_tokens, norm_g, w_in, b_f, sinks, q_norm_g, kv_norm_g, w_uq, w_ukv, w_br, w_out, final_norm_g):
    batch, n_real, _ = x.shape
    seq = BLK + n_real
    depth = w_in.shape[0]
    assert (seq - BLK) % (Q_SUB * CHAINS_PER_GROUP) == 0 and K_TILE % Q_SUB == 0 and (batch * seq) % PRE_TM == 0 and (batch * seq) % POST_TM == 0
    h = jnp.concatenate([jnp.zeros((batch, PAD, D_MODEL), x.dtype),
                         jnp.broadcast_to(meta_tokens[None].astype(x.dtype), (batch, N_META, D_MODEL)),
                         x], axis=1).reshape(batch * seq, D_MODEL)
    dcap, mqcap = _dense_caps()
    consts = {"aug_e": _aug_matrix(), "tri": jnp.asarray(np.tri(PRE_TM), BF16),
              "rope_tab": _rope_tables(seq, PRE_TM), "dcap": dcap, "mqcap": mqcap}
    swa_bias = _swa_biases()
    final_g = final_norm_g.reshape(1, D_MODEL)
    lw = _stacked_weights(norm_g, w_in, b_f, sinks, q_norm_g, kv_norm_g, w_uq, w_ukv, w_br, w_out)
    for l in range(depth):
        qk, vt, qa, ka, vta = _pre_call(h, lw, l, consts, seq=seq)
        oa = _swa_call(qa, ka, vta, swa_bias, lw["sink"], l, batch=batch, seq=seq)
        obc = _dense_call(qk, vt, consts, batch=batch, seq=seq)
        h = _post_call(h, oa, obc, lw, l, final_g, final=(l == depth - 1), batch=batch, seq=seq)
    return h.reshape(batch, n_real, D_MODEL)
```

```python
import functools
import math

import jax
import jax.numpy as jnp
import numpy as np
from jax import lax
from jax.experimental import pallas as pl
from jax.experimental.pallas import tpu as pltpu

D_MODEL = 1024
N_META = 16
BLK = 128
PAD = BLK - N_META
HEAD_DIM = 64
N_HEADS = 8
A_KV_HEADS = 2
C_Q_RANK = 256
C_KV_RANK = 128
C_NOPE = 64
C_ROPE = 32
ROPE_HALF = C_ROPE // 2
ROPE_THETA = 10000.0
N_BRANCH = 3
BRANCH_W = 512
EPS = 1e-6
NEG = -1e30
LOG2E = math.log2(math.e)

LANES = 128
Q_SUB = 512
K_TILE = 512
CHAINS_PER_GROUP = 1
LOOKAHEAD = 2
HEADS_PER_ITER = 2
SUM_ROWS = 16
PRE_TM = 512
POST_TM = 1024
VMEM_LIMIT_BYTES = 56 * 1024 * 1024

SCALE_AB = HEAD_DIM ** -0.5 * LOG2E
SCALE_C = (C_NOPE + C_ROPE) ** -0.5 * LOG2E

COL_AQ = 0
COL_AK = 512
COL_AV = 640
COL_BQ = 768
COL_BK = 1280
COL_BV = 1792
COL_CQ = 2304
COL_CKV = 2560
COL_MISC = 2688
N_PRE = 2816
MISC_ROPE = 64
AUG_ONE = 24
AUG_W = 6
AUG_K0 = N_HEADS * AUG_W

BF16 = jnp.bfloat16
F32 = jnp.float32


def _dot(a, b):
    return jnp.dot(a, b, preferred_element_type=F32)


def _dot_nt(a, b):
    return lax.dot_general(a, b, (((1,), (1,)), ((), ())), preferred_element_type=F32)


def _rms(x, g):
    return x * lax.rsqrt(jnp.mean(x * x, axis=-1, keepdims=True) + EPS) * g


def _split3(x):
    hi = x.astype(BF16).astype(F32)
    r = x - hi
    mid = r.astype(BF16).astype(F32)
    lo = (r - mid).astype(BF16).astype(F32)
    return hi, mid, lo


def _pre_kernel(x_ref, g_ref, w1_ref, wuq_ref, wuk_ref, wuv_ref, e_ref, tri_ref, qng_ref, kvng_ref,
                bf_ref, tab_ref, qk_ref, vt_ref, qa_ref, ka_ref, vta_ref, carry_ref, *, tm, seq):
    i = pl.program_id(0)
    hb = _rms(x_ref[...], g_ref[...]).astype(BF16)
    lane = lax.broadcasted_iota(jnp.int32, (tm, LANES), 1)
    row = lax.broadcasted_iota(jnp.int32, (tm, LANES), 0)
    lo_half = lane < HEAD_DIM

    misc = _dot(hb, w1_ref[:, COL_MISC:N_PRE])
    rc = _dot(hb, w1_ref[:, COL_CQ:COL_MISC])
    ra = _dot(hb, w1_ref[:, COL_AQ:COL_BQ])

    for j in range(N_HEADS // 2):
        pq = ra[:, LANES * j:LANES * (j + 1)] * SCALE_AB
        qa_ref[j] = jnp.where(lo_half, pq, 0.0).astype(BF16)
        qa_ref[j + N_HEADS // 2] = jnp.where(lo_half, 0.0, pq).astype(BF16)
    ka_ref[...] = ra[:, COL_AK:COL_AV].astype(BF16)
    vta_ref[...] = ra[:, COL_AV:COL_BQ].T.astype(BF16)

    p0 = lax.rem(i * tm, seq)
    pib = p0 + row
    pib = jnp.where(pib >= seq, pib - seq, pib)
    xg = misc + bf_ref[...]
    logf = (jnp.minimum(xg, 0.0) - jnp.log1p(jnp.exp(-jnp.abs(xg)))) * LOG2E
    logf = jnp.where((pib >= PAD) & (lane < N_HEADS), logf, 0.0)

    def pieces(v):
        rep = v + pltpu.roll(v, N_HEADS, 1) + pltpu.roll(v, 2 * N_HEADS, 1)
        hi, mid, lo = _split3(rep)
        return jnp.where(lane < N_HEADS, hi, jnp.where(lane < 2 * N_HEADS, mid, lo))

    def gather_pieces(v):
        return v + pltpu.roll(v, LANES - N_HEADS, 1) + pltpu.roll(v, LANES - 2 * N_HEADS, 1)

    bnd = seq - p0
    cum = gather_pieces(_dot(tri_ref[...], pieces(logf).astype(BF16)))
    before = jnp.sum(jnp.where(row < bnd, logf, 0.0), axis=0, keepdims=True)

    def rope(v, c, s1, s2):
        return v * c + pltpu.roll(v, LANES - ROPE_HALF, 1) * s1 + pltpu.roll(v, ROPE_HALF, 1) * s2

    cqn = _rms(rc[:, :C_Q_RANK], qng_ref[...]).astype(BF16)
    ckvn = _rms(rc[:, C_Q_RANK:], kvng_ref[...]).astype(BF16)
    qc = _dot(cqn, wuq_ref[...])
    for h in range(N_HEADS):
        blk = qc[:, LANES * h:LANES * (h + 1)]
        qk_ref[N_HEADS + h] = rope(blk, tab_ref[0], tab_ref[1], tab_ref[2]).astype(BF16)

    kc = _dot(ckvn, wuk_ref[...])
    kr = rope(misc, tab_ref[3], tab_ref[4], tab_ref[5])
    for h in range(N_HEADS):
        qk_ref[3 * N_HEADS + h] = jnp.where(lo_half, kc[:, LANES * h:LANES * (h + 1)], kr).astype(BF16)
    vt_ref[BRANCH_W:2 * BRANCH_W, :] = _dot(ckvn, wuv_ref[...]).T.astype(BF16)

    vt_ref[0:BRANCH_W, :] = _dot(hb, w1_ref[:, COL_BV:COL_CQ]).T.astype(BF16)

    @pl.when(i == 0)
    def _():
        carry_ref[...] = jnp.zeros_like(carry_ref)

    carry_in = jnp.where(p0 == 0, 0.0, carry_ref[0:1, :])
    cum = jnp.where(lane < N_HEADS, cum + jnp.where(row >= bnd, -before, carry_in), 0.0)
    carry_ref[...] = jnp.broadcast_to(cum[tm - 1:tm, :], carry_ref.shape)
    pc = jnp.where(lane == AUG_ONE, 1.0, pieces(cum)).astype(BF16)
    aug = _dot(pc, e_ref[...])

    rb = _dot(hb, w1_ref[:, COL_BQ:COL_BV])
    for h in range(N_HEADS):
        j, t = divmod(h, 2)
        own = lo_half if t == 0 else jnp.logical_not(lo_half)
        a = HEAD_DIM if t == 0 else 0
        in_aug = (lane >= a) & (lane < a + AUG_W)
        pq = rb[:, LANES * j:LANES * (j + 1)] * SCALE_AB
        pk = rb[:, BRANCH_W + LANES * j:BRANCH_W + LANES * (j + 1)]
        aq = pltpu.roll(aug, (a - AUG_W * h) % LANES, 1)
        ak = pltpu.roll(aug, (a - AUG_K0 - AUG_W * h) % LANES, 1)
        qk_ref[h] = jnp.where(own, pq, jnp.where(in_aug, aq, 0.0)).astype(BF16)
        qk_ref[2 * N_HEADS + h] = jnp.where(own, pk, jnp.where(in_aug, ak, 0.0)).astype(BF16)


def _const_spec(shape):
    nd = len(shape)
    return pl.BlockSpec(shape, lambda *_: (0,) * nd, pipeline_mode=pl.Buffered(1))


def _layer_spec(shape, l):
    nd = len(shape)
    return pl.BlockSpec((None,) + shape, lambda *_: (l,) + (0,) * nd, pipeline_mode=pl.Buffered(1))


def _pre_call(x, lw, l, consts, *, seq):
    m = x.shape[0]
    tm = PRE_TM
    n_tiles = m // tm
    period = seq // LANES
    grid_spec = pl.GridSpec(
        grid=(n_tiles,),
        in_specs=[
            pl.BlockSpec((tm, D_MODEL), lambda i: (i, 0)),
            _layer_spec((1, D_MODEL), l),
            _layer_spec((D_MODEL, N_PRE), l),
            _layer_spec((C_Q_RANK, N_HEADS * LANES), l),
            _layer_spec((C_KV_RANK, N_HEADS * LANES), l),
            _layer_spec((C_KV_RANK, BRANCH_W), l),
            _const_spec((LANES, LANES)),
            _const_spec((tm, tm)),
            _layer_spec((1, C_Q_RANK), l),
            _layer_spec((1, C_KV_RANK), l),
            _layer_spec((1, LANES), l),
            pl.BlockSpec((6, tm, LANES), lambda i: (0, i % period, 0)),
        ],
        out_specs=[
            pl.BlockSpec((4 * N_HEADS, tm, LANES), lambda i: (0, i, 0)),
            pl.BlockSpec((2 * BRANCH_W, tm), lambda i: (0, i)),
            pl.BlockSpec((N_HEADS, tm, LANES), lambda i: (0, i, 0)),
            pl.BlockSpec((tm, LANES), lambda i: (i, 0)),
            pl.BlockSpec((LANES, tm), lambda i: (0, i)),
        ],
        scratch_shapes=[pltpu.VMEM((8, LANES), F32)],
    )
    out_shape = [
        jax.ShapeDtypeStruct((4 * N_HEADS, m, LANES), BF16),
        jax.ShapeDtypeStruct((2 * BRANCH_W, m), BF16),
        jax.ShapeDtypeStruct((N_HEADS, m, LANES), BF16),
        jax.ShapeDtypeStruct((m, LANES), BF16),
        jax.ShapeDtypeStruct((LANES, m), BF16),
    ]
    return pl.pallas_call(
        functools.partial(_pre_kernel, tm=tm, seq=seq),
        grid_spec=grid_spec,
        out_shape=out_shape,
        compiler_params=pltpu.CompilerParams(
            dimension_semantics=("arbitrary",), vmem_limit_bytes=VMEM_LIMIT_BYTES),
        name="pre",
    )(x, lw["norm_g"], lw["w1"], lw["wuq"], lw["wuk"], lw["wuv"], consts["aug_e"], consts["tri"],
      lw["q_norm_g"], lw["kv_norm_g"], lw["b_f"], consts["rope_tab"])


def _swa_kernel(q_ref, k_ref, vt_ref, bias_ref, sink_ref, o_ref, ot_scr, *, seq):
    nb = seq // BLK
    half = N_HEADS // 2 * BLK
    sink = sink_ref[...]
    def scores(n):
        q = q_ref[:, n * BLK:(n + 1) * BLK, :].reshape(N_HEADS * BLK, LANES)
        return _dot_nt(k_ref[max(n - 1, 0) * BLK:(n + 1) * BLK, :], q)

    s_next = scores(0)
    for n in range(nb):
        k0 = max(n - 1, 0) * BLK
        k1 = (n + 1) * BLK
        if n == 0:
            bias = bias_ref[2, BLK:, :]
        else:
            bias = bias_ref[0 if n >= 2 else 1]
        s = jnp.minimum(s_next, -NEG) + bias
        if n + 1 < nb:
            s_next = scores(n + 1)
        m = jnp.maximum(jnp.max(s, axis=0, keepdims=True), sink)
        pb = jnp.exp2(s - m).astype(BF16)
        sink_p = jnp.exp2(sink - m)
        ones = jnp.ones((SUM_ROWS, k1 - k0), BF16)
        for kv in range(A_KV_HEADS):
            vt = jnp.concatenate([vt_ref[kv * HEAD_DIM:(kv + 1) * HEAD_DIM, k0:k1], ones], axis=0)
            pv = _dot(vt, pb[:, kv * half:(kv + 1) * half])
            pv = pv[:HEAD_DIM] / (pv[HEAD_DIM:HEAD_DIM + 1] + sink_p[:, kv * half:(kv + 1) * half])
            for g in range(N_HEADS // A_KV_HEADS):
                h = kv * (N_HEADS // A_KV_HEADS) + g
                ot_scr[h * HEAD_DIM:(h + 1) * HEAD_DIM, n * BLK:(n + 1) * BLK] = pv[:, g * BLK:(g + 1) * BLK]
    for c in range(nb):
        o_ref[c * BLK:(c + 1) * BLK, :] = ot_scr[:, c * BLK:(c + 1) * BLK].T.astype(o_ref.dtype)


def _swa_call(qa, ka, vta, bias, sink, l, *, batch, seq):
    m = ka.shape[0]
    grid_spec = pl.GridSpec(
        grid=(batch,),
        in_specs=[
            pl.BlockSpec((N_HEADS, seq, LANES), lambda b: (0, b, 0)),
            pl.BlockSpec((seq, LANES), lambda b: (b, 0)),
            pl.BlockSpec((LANES, seq), lambda b: (0, b)),
            _const_spec((3, 2 * BLK, N_HEADS * BLK)),
            _layer_spec((1, N_HEADS * BLK), l),
        ],
        out_specs=pl.BlockSpec((seq, BRANCH_W), lambda b: (b, 0)),
        scratch_shapes=[pltpu.VMEM((BRANCH_W, seq), F32)],
    )
    return pl.pallas_call(
        functools.partial(_swa_kernel, seq=seq),
        grid_spec=grid_spec,
        out_shape=jax.ShapeDtypeStruct((m, BRANCH_W), BF16),
        compiler_params=pltpu.CompilerParams(
            dimension_semantics=("parallel",), vmem_limit_bytes=VMEM_LIMIT_BYTES),
        name="swa",
    )(qa, ka, vta, bias, sink)


def _dense_kernel(q_ref, k_ref, vt_ref, dcap_ref, mqcap_ref, o_ref, ot_scr, *, seq):
    def head_chains(c):
        groups = [[(c, 0, BLK, [([(0, BLK)], [(0, BLK)], [(mqcap_ref, 0, BLK)], 0)])]]
        for u in range((seq - BLK) // Q_SUB):
            q0 = BLK + Q_SUB * u
            n_full = u * Q_SUB // K_TILE
            steps = [([(BLK + K_TILE * j, K_TILE)],) * 2 + (None, 0) for j in range(n_full)]
            rows = q0 + Q_SUB - (BLK + K_TILE * n_full)
            diag = (BLK + K_TILE * n_full, rows)
            steps.append(([(PAD, N_META), diag], [(0, BLK), diag],
                          [(dcap_ref, PAD, N_META), (dcap_ref, BLK + K_TILE - rows, rows)], PAD))
            if u % CHAINS_PER_GROUP == 0:
                groups.append([])
            groups[-1].append((c, q0, Q_SUB, steps))
        return groups

    def gather(ref_rows, ranges, axis):
        parts = [ref_rows(a, n) for a, n in ranges]
        return parts[0] if len(parts) == 1 else jnp.concatenate(parts, axis=axis)

    def head_body(hg, carry):
        heads = [hg * HEADS_PER_ITER + c for c in range(HEADS_PER_ITER)]
        rows = [pl.ds(pl.multiple_of(h * HEAD_DIM, HEAD_DIM), HEAD_DIM) for h in heads]
        per_head = [head_chains(c) for c in range(HEADS_PER_ITER)]
        slots = []
        for gi in range(len(per_head[0])):
            group = [ch for groups in per_head for ch in groups[gi]]
            for t in range(len(group[0][3])):
                slots += [(ch, t) for ch in group]

        def scores(slot):
            (c, q0, qn, steps), t = slot
            keys = gather(lambda a, n: k_ref[heads[c], a:a + n, :], steps[t][0], 0)
            return _dot_nt(keys, q_ref[heads[c], q0:q0 + qn, :])

        ahead = [scores(sl) for sl in slots[:LOOKAHEAD]]
        state = {}
        for n, ((c, q0, qn, steps), t) in enumerate(slots):
            s = ahead.pop(0)
            if n + LOOKAHEAD < len(slots):
                ahead.append(scores(slots[n + LOOKAHEAD]))
            _, value_ranges, cap_ranges, zero_rows = steps[t]
            if cap_ranges is not None:
                cap = [ref[a:a + n, :] for ref, a, n in cap_ranges]
                s = jnp.minimum(s, cap[0] if len(cap) == 1 else jnp.concatenate(cap, axis=0))
            mc = jnp.max(s, axis=0, keepdims=True)
            prev = state.get((c, q0))
            m_new = mc if prev is None else jnp.maximum(prev[0], mc)
            pt = jnp.exp2(s - m_new).astype(BF16)
            if zero_rows:
                pt = jnp.concatenate([jnp.zeros((zero_rows, qn), BF16), pt], axis=0)
            vt = gather(lambda a, n: vt_ref[rows[c], a:a + n], value_ranges, 1)
            vt = jnp.concatenate([vt, jnp.ones((SUM_ROWS, vt.shape[1]), BF16)], axis=0)
            pv = _dot(vt, pt)
            acc = pv if prev is None else jnp.exp2(prev[0] - m_new) * prev[1] + pv
            state[c, q0] = (m_new, acc)
            if t == len(steps) - 1:
                del state[c, q0]
                ot_scr[rows[c], q0:q0 + qn] = acc[:HEAD_DIM] / acc[HEAD_DIM:HEAD_DIM + 1]
        return carry

    lax.fori_loop(0, N_HEADS // HEADS_PER_ITER, head_body, 0)
    for c in range(seq // BLK):
        o_ref[0, c * BLK:(c + 1) * BLK, :] = ot_scr[:, c * BLK:(c + 1) * BLK].T.astype(o_ref.dtype)


def _dense_call(qk, vt, consts, *, batch, seq):
    m = qk.shape[1]
    grid_spec = pl.GridSpec(
        grid=(batch, 2),
        in_specs=[
            pl.BlockSpec((N_HEADS, seq, LANES), lambda b, x: (x, b, 0)),
            pl.BlockSpec((N_HEADS, seq, LANES), lambda b, x: (2 + x, b, 0)),
            pl.BlockSpec((BRANCH_W, seq), lambda b, x: (x, b)),
            _const_spec((BLK + K_TILE, Q_SUB)),
            _const_spec((BLK, BLK)),
        ],
        out_specs=pl.BlockSpec((1, seq, BRANCH_W), lambda b, x: (x, b, 0)),
        scratch_shapes=[pltpu.VMEM((BRANCH_W, seq), F32)],
    )
    return pl.pallas_call(
        functools.partial(_dense_kernel, seq=seq),
        grid_spec=grid_spec,
        out_shape=jax.ShapeDtypeStruct((2, m, BRANCH_W), BF16),
        compiler_params=pltpu.CompilerParams(
            dimension_semantics=("parallel", "parallel"), vmem_limit_bytes=VMEM_LIMIT_BYTES),
        name="dense",
    )(qk, qk, vt, consts["dcap"], consts["mqcap"])


def _post_kernel(x_ref, oa_ref, obc_ref, g_ref, wz_ref, wg_ref, wbr_ref, wout_ref, fg_ref, out_ref, *, final):
    x = x_ref[...]
    hb = _rms(x, g_ref[...]).astype(BF16)
    y = None
    for i in range(N_BRANCH):
        o = (oa_ref[...] if i == 0 else obc_ref[i - 1]).astype(F32)
        z = _dot(hb, wz_ref[:, BRANCH_W * i:BRANCH_W * (i + 1)])
        gate = jax.nn.sigmoid(_dot(hb, wg_ref[:, D_MODEL * i:D_MODEL * (i + 1)]))
        u = (o * (z * jax.nn.sigmoid(z))).astype(BF16)
        t = _dot(u, wbr_ref[i])
        y = gate * t if y is None else y + gate * t
    out = x + _dot(y.astype(BF16), wout_ref[...])
    if final:
        out = _rms(out, fg_ref[...])
    out_ref[...] = out


def _post_call(x, oa, obc, lw, l, final_g, *, final, batch, seq):
    m = x.shape[0]
    tm = POST_TM
    if final:
        n_t = (seq - BLK) // tm
        grid = (batch, n_t)
        m_out = batch * (seq - BLK)

        def row0(b, t):
            return pl.multiple_of(b * seq + BLK + t * tm, BLK)

        el = pl.Element
        stream_specs = [
            pl.BlockSpec((el(tm), el(D_MODEL)), lambda b, t: (row0(b, t), 0)),
            pl.BlockSpec((el(tm), el(BRANCH_W)), lambda b, t: (row0(b, t), 0)),
            pl.BlockSpec((el(2), el(tm), el(BRANCH_W)), lambda b, t: (0, row0(b, t), 0)),
        ]
        out_spec = pl.BlockSpec((tm, D_MODEL), lambda b, t: (b * n_t + t, 0))
    else:
        grid = (m // tm,)
        m_out = m
        stream_specs = [
            pl.BlockSpec((tm, D_MODEL), lambda i: (i, 0)),
            pl.BlockSpec((tm, BRANCH_W), lambda i: (i, 0)),
            pl.BlockSpec((2, tm, BRANCH_W), lambda i: (0, i, 0)),
        ]
        out_spec = pl.BlockSpec((tm, D_MODEL), lambda i: (i, 0))
    grid_spec = pl.GridSpec(
        grid=grid,
        in_specs=stream_specs + [
            _layer_spec((1, D_MODEL), l),
            _layer_spec((D_MODEL, N_BRANCH * BRANCH_W), l),
            _layer_spec((D_MODEL, N_BRANCH * D_MODEL), l),
            _layer_spec((N_BRANCH, BRANCH_W, D_MODEL), l),
            _layer_spec((D_MODEL, D_MODEL), l),
            _const_spec((1, D_MODEL)),
        ],
        out_specs=out_spec,
    )
    return pl.pallas_call(
        functools.partial(_post_kernel, final=final),
        grid_spec=grid_spec,
        out_shape=jax.ShapeDtypeStruct((m_out, D_MODEL), F32),
        input_output_aliases={} if final else {0: 0},
        compiler_params=pltpu.CompilerParams(
            dimension_semantics=("parallel",) * len(grid), vmem_limit_bytes=VMEM_LIMIT_BYTES),
        name="post",
    )(x, oa, obc, lw["norm_g"], lw["wz"], lw["wg"], lw["w_br"], lw["w_out"], final_g)


def _rope_tables(seq, tm):
    pos = (jnp.arange(seq) - PAD).astype(F32)
    inv = ROPE_THETA ** (-jnp.arange(ROPE_HALF, dtype=F32) / ROPE_HALF)
    ang = pos[:, None] * inv[None, :]
    cos, sin = jnp.cos(ang), jnp.sin(ang)
    zero = jnp.zeros_like(cos)

    def lanes(nope, a, b):
        return jnp.concatenate([jnp.full((seq, C_NOPE), nope, F32), a, b,
                                jnp.zeros((seq, LANES - C_NOPE - C_ROPE), F32)], axis=1)

    tabs = jnp.stack([lanes(1.0, cos, cos) * SCALE_C, lanes(0.0, -sin, zero) * SCALE_C,
                      lanes(0.0, zero, sin) * SCALE_C,
                      lanes(0.0, cos, cos), lanes(0.0, -sin, zero), lanes(0.0, zero, sin)])
    return jnp.tile(tabs, (1, tm // LANES, 1))


def _aug_matrix():
    e = np.zeros((LANES, LANES), np.float32)
    for h in range(N_HEADS):
        q0, k0 = AUG_W * h, AUG_K0 + AUG_W * h
        for p in range(3):
            e[p * N_HEADS + h, q0 + p] = 1.0
            e[AUG_ONE, q0 + 3 + p] = 1.0
            e[AUG_ONE, k0 + p] = 1.0
            e[p * N_HEADS + h, k0 + 3 + p] = -1.0
    return jnp.asarray(e, BF16)


def _dense_caps():
    meta_ok = (np.arange(BLK)[:, None] >= PAD) & np.ones((1, Q_SUB), bool)
    kk = np.arange(K_TILE)[:, None] - (K_TILE - Q_SUB)
    diag_ok = kk <= np.arange(Q_SUB)[None, :]
    dcap = np.where(np.concatenate([meta_ok, diag_ok]), -NEG, NEG)
    km = np.arange(BLK)[:, None]
    mqcap = np.where((km >= PAD) & (km <= np.arange(BLK)[None, :]), -NEG, NEG)
    return jnp.asarray(dcap, F32), jnp.asarray(mqcap, F32)


def _swa_biases():
    slopes = 2.0 ** (-8.0 * (jnp.arange(N_HEADS, dtype=F32) + 1.0) / N_HEADS)
    s = np.arange(2 * BLK)[:, None]
    t = np.arange(BLK)[None, :]
    dist = BLK + t - s
    window = (dist >= 0) & (dist < BLK)
    tabs = []
    for mask in (window, window & (s >= PAD), window & (s >= BLK + PAD)):
        per_head = [jnp.where(mask, -slopes[h] * dist.astype(np.float32) * LOG2E, NEG) for h in range(N_HEADS)]
        tabs.append(jnp.concatenate(per_head, axis=1))
    return jnp.stack(tabs).astype(F32)


def _stacked_weights(norm_g, w_in, b_f, sinks, q_norm_g, kv_norm_g, w_uq, w_ukv, w_br, w_out):
    depth = w_in.shape[0]
    splits = np.cumsum([0, 512, 128, 128, 512, 512, 512, 8, 256, 128, 32, 1536, 3072])
    aq, ak, av, bq, bk, bv, bfl, cq, ckv, ckr, z, g = [w_in[:, :, splits[i]:splits[i + 1]] for i in range(12)]
    aq = aq.reshape(depth, D_MODEL, A_KV_HEADS, N_HEADS // A_KV_HEADS, HEAD_DIM)
    aq = jnp.transpose(aq, (0, 1, 3, 2, 4)).reshape(depth, D_MODEL, N_HEADS * HEAD_DIM)
    misc = jnp.concatenate([bfl, jnp.zeros((depth, D_MODEL, MISC_ROPE - N_HEADS), F32), ckr,
                            jnp.zeros((depth, D_MODEL, LANES - MISC_ROPE - C_ROPE), F32)], axis=2)
    w1 = jnp.concatenate([aq, ak, av, bq, bk, bv, cq, ckv, misc], axis=2).astype(BF16)
    wuq = jnp.pad(w_uq.reshape(depth, C_Q_RANK, N_HEADS, C_NOPE + C_ROPE),
                  ((0, 0), (0, 0), (0, 0), (0, LANES - C_NOPE - C_ROPE))).reshape(depth, C_Q_RANK, N_HEADS * LANES)
    wukv = w_ukv.reshape(depth, C_KV_RANK, N_HEADS, 2 * HEAD_DIM)
    wuk = jnp.pad(wukv[..., :C_NOPE], ((0, 0), (0, 0), (0, 0), (0, LANES - C_NOPE)))
    wuk = wuk.reshape(depth, C_KV_RANK, N_HEADS * LANES)
    wuv = wukv[..., C_NOPE:].reshape(depth, C_KV_RANK, BRANCH_W)
    return {
        "norm_g": norm_g.reshape(depth, 1, D_MODEL), "w1": w1, "wuq": wuq.astype(BF16), "wuk": wuk.astype(BF16),
        "wuv": wuv.astype(BF16), "q_norm_g": q_norm_g.reshape(depth, 1, C_Q_RANK),
        "kv_norm_g": kv_norm_g.reshape(depth, 1, C_KV_RANK),
        "b_f": jnp.pad(b_f, ((0, 0), (0, LANES - N_HEADS))).reshape(depth, 1, LANES),
        "sink": jnp.repeat(sinks * LOG2E, BLK, axis=1).reshape(depth, 1, N_HEADS * BLK),
        "wz": z.astype(BF16), "wg": g.astype(BF16), "w_br": w_br.astype(BF16), "w_out": w_out.astype(BF16),
    }


def kernel(x, meta_tokens, norm_g, w_in, b_f, sinks, q_norm_g, kv_norm_g, w_uq, w_ukv, w_br, w_out, final_norm_g):
    batch, n_real, _ = x.shape
    seq = BLK + n_real
    depth = w_in.shape[0]
    assert (seq - BLK) % (Q_SUB * CHAINS_PER_GROUP) == 0 and K_TILE % Q_SUB == 0 and (batch * seq) % PRE_TM == 0 and (batch * seq) % POST_TM == 0
    h = jnp.concatenate([jnp.zeros((batch, PAD, D_MODEL), x.dtype),
                         jnp.broadcast_to(meta_tokens[None].astype(x.dtype), (batch, N_META, D_MODEL)),
                         x], axis=1).reshape(batch * seq, D_MODEL)
    dcap, mqcap = _dense_caps()
    consts = {"aug_e": _aug_matrix(), "tri": jnp.asarray(np.tri(PRE_TM), BF16),
              "rope_tab": _rope_tables(seq, PRE_TM), "dcap": dcap, "mqcap": mqcap}
    swa_bias = _swa_biases()
    final_g = final_norm_g.reshape(1, D_MODEL)
    lw = _stacked_weights(norm_g, w_in, b_f, sinks, q_norm_g, kv_norm_g, w_uq, w_ukv, w_br, w_out)
    for l in range(depth):
        qk, vt, qa, ka, vta = _pre_call(h, lw, l, consts, seq=seq)
        oa = _swa_call(qa, ka, vta, swa_bias, lw["sink"], l, batch=batch, seq=seq)
        obc = _dense_call(qk, vt, consts, batch=batch, seq=seq)
        h = _post_call(h, oa, obc, lw, l, final_g, final=(l == depth - 1), batch=batch, seq=seq)
    return h.reshape(batch, n_real, D_MODEL)
```

```python
import functools
import math

import jax
import jax.numpy as jnp
import numpy as np
from jax import lax
from jax.experimental import pallas as pl
from jax.experimental.pallas import tpu as pltpu

D_MODEL = 1024
N_META = 16
BLK = 128
PAD = BLK - N_META
HEAD_DIM = 64
N_HEADS = 8
A_KV_HEADS = 2
C_Q_RANK = 256
C_KV_RANK = 128
C_NOPE = 64
C_ROPE = 32
ROPE_HALF = C_ROPE // 2
ROPE_THETA = 10000.0
N_BRANCH = 3
BRANCH_W = 512
EPS = 1e-6
NEG = -1e30
LOG2E = math.log2(math.e)

LANES = 128
Q_SUB = 512
K_TILE = 512
CHAINS_PER_GROUP = 1
LOOKAHEAD = 2
HEADS_PER_ITER = 2
SUM_ROWS = 16
PRE_TM = 512
POST_TM = 1024
VMEM_LIMIT_BYTES = 56 * 1024 * 1024

SCALE_AB = HEAD_DIM ** -0.5 * LOG2E
SCALE_C = (C_NOPE + C_ROPE) ** -0.5 * LOG2E

COL_AQ = 0
COL_AK = 512
COL_AV = 640
COL_BQ = 768
COL_BK = 1280
COL_BV = 1792
COL_CQ = 2304
COL_CKV = 2560
COL_MISC = 2688
N_PRE = 2816
MISC_ROPE = 64
AUG_ONE = 24
AUG_W = 6
AUG_K0 = N_HEADS * AUG_W

BF16 = jnp.bfloat16
F32 = jnp.float32


def _dot(a, b):
    return jnp.dot(a, b, preferred_element_type=F32)


def _dot_nt(a, b):
    return lax.dot_general(a, b, (((1,), (1,)), ((), ())), preferred_element_type=F32)


def _rms(x, g):
    return x * lax.rsqrt(jnp.mean(x * x, axis=-1, keepdims=True) + EPS) * g


def _split3(x):
    hi = x.astype(BF16).astype(F32)
    r = x - hi
    mid = r.astype(BF16).astype(F32)
    lo = (r - mid).astype(BF16).astype(F32)
    return hi, mid, lo


def _pre_kernel(x_ref, g_ref, w1_ref, wuq_ref, wuk_ref, wuv_ref, e_ref, tri_ref, qng_ref, kvng_ref,
                bf_ref, tab_ref, qk_ref, vt_ref, qa_ref, ka_ref, vta_ref, carry_ref, *, tm, seq):
    i = pl.program_id(0)

    @pl.when(i == 0)
    def _():
        carry_ref[...] = jnp.zeros_like(carry_ref)

    hb = _rms(x_ref[...], g_ref[...]).astype(BF16)
    lane = lax.broadcasted_iota(jnp.int32, (tm, LANES), 1)
    row = lax.broadcasted_iota(jnp.int32, (tm, LANES), 0)
    lo_half = lane < HEAD_DIM

    misc = _dot(hb, w1_ref[:, COL_MISC:N_PRE])
    rc = _dot(hb, w1_ref[:, COL_CQ:COL_MISC])
    ra = _dot(hb, w1_ref[:, COL_AQ:COL_BQ])

    for j in range(N_HEADS // 2):
        pq = ra[:, LANES * j:LANES * (j + 1)] * SCALE_AB
        qa_ref[j] = jnp.where(lo_half, pq, 0.0).astype(BF16)
        qa_ref[j + N_HEADS // 2] = jnp.where(lo_half, 0.0, pq).astype(BF16)
    ka_ref[...] = ra[:, COL_AK:COL_AV].astype(BF16)
    vta_ref[...] = ra[:, COL_AV:COL_BQ].T.astype(BF16)

    p0 = lax.rem(i * tm, seq)
    pib = p0 + row
    pib = jnp.where(pib >= seq, pib - seq, pib)
    xg = misc + bf_ref[...]
    logf = (jnp.minimum(xg, 0.0) - jnp.log1p(jnp.exp(-jnp.abs(xg)))) * LOG2E
    logf = jnp.where((pib >= PAD) & (lane < N_HEADS), logf, 0.0)

    def pieces(v):
        rep = v + pltpu.roll(v, N_HEADS, 1) + pltpu.roll(v, 2 * N_HEADS, 1)
        hi, mid, lo = _split3(rep)
        return jnp.where(lane < N_HEADS, hi, jnp.where(lane < 2 * N_HEADS, mid, lo))

    def gather_pieces(v):
        return v + pltpu.roll(v, LANES - N_HEADS, 1) + pltpu.roll(v, LANES - 2 * N_HEADS, 1)

    bnd = seq - p0
    cum = gather_pieces(_dot(tri_ref[...], pieces(logf).astype(BF16)))
    before = jnp.sum(jnp.where(row < bnd, logf, 0.0), axis=0, keepdims=True)

    def rope(v, c, s1, s2):
        return v * c + pltpu.roll(v, LANES - ROPE_HALF, 1) * s1 + pltpu.roll(v, ROPE_HALF, 1) * s2

    cqn = _rms(rc[:, :C_Q_RANK], qng_ref[...]).astype(BF16)
    ckvn = _rms(rc[:, C_Q_RANK:], kvng_ref[...]).astype(BF16)
    qc = _dot(cqn, wuq_ref[...])
    for h in range(N_HEADS):
        blk = qc[:, LANES * h:LANES * (h + 1)]
        qk_ref[N_HEADS + h] = rope(blk, tab_ref[0], tab_ref[1], tab_ref[2]).astype(BF16)

    kc = _dot(ckvn, wuk_ref[...])
    kr = rope(misc, tab_ref[3], tab_ref[4], tab_ref[5])
    for h in range(N_HEADS):
        qk_ref[3 * N_HEADS + h] = jnp.where(lo_half, kc[:, LANES * h:LANES * (h + 1)], kr).astype(BF16)
    vt_ref[BRANCH_W:2 * BRANCH_W, :] = _dot(ckvn, wuv_ref[...]).T.astype(BF16)

    vt_ref[0:BRANCH_W, :] = _dot(hb, w1_ref[:, COL_BV:COL_CQ]).T.astype(BF16)

    carry_in = jnp.where(p0 == 0, 0.0, carry_ref[0:1, :])
    cum = jnp.where(lane < N_HEADS, cum + jnp.where(row >= bnd, -before, carry_in), 0.0)
    carry_ref[...] = jnp.broadcast_to(cum[tm - 1:tm, :], carry_ref.shape)
    pc = jnp.where(lane == AUG_ONE, 1.0, pieces(cum)).astype(BF16)
    aug = _dot(pc, e_ref[...])

    rb = _dot(hb, w1_ref[:, COL_BQ:COL_BV])
    for h in range(N_HEADS):
        j, t = divmod(h, 2)
        own = lo_half if t == 0 else jnp.logical_not(lo_half)
        a = HEAD_DIM if t == 0 else 0
        in_aug = (lane >= a) & (lane < a + AUG_W)
        pq = rb[:, LANES * j:LANES * (j + 1)] * SCALE_AB
        pk = rb[:, BRANCH_W + LANES * j:BRANCH_W + LANES * (j + 1)]
        aq = pltpu.roll(aug, (a - AUG_W * h) % LANES, 1)
        ak = pltpu.roll(aug, (a - AUG_K0 - AUG_W * h) % LANES, 1)
        qk_ref[h] = jnp.where(own, pq, jnp.where(in_aug, aq, 0.0)).astype(BF16)
        qk_ref[2 * N_HEADS + h] = jnp.where(own, pk, jnp.where(in_aug, ak, 0.0)).astype(BF16)


def _const_spec(shape):
    nd = len(shape)
    return pl.BlockSpec(shape, lambda *_: (0,) * nd, pipeline_mode=pl.Buffered(1))


def _layer_spec(shape, l):
    nd = len(shape)
    return pl.BlockSpec((None,) + shape, lambda *_: (l,) + (0,) * nd, pipeline_mode=pl.Buffered(1))


def _pre_call(x, lw, l, consts, *, seq):
    m = x.shape[0]
    tm = PRE_TM
    n_tiles = m // tm
    period = seq // LANES
    grid_spec = pl.GridSpec(
        grid=(n_tiles,),
        in_specs=[
            pl.BlockSpec((tm, D_MODEL), lambda i: (i, 0)),
            _layer_spec((1, D_MODEL), l),
            _layer_spec((D_MODEL, N_PRE), l),
            _layer_spec((C_Q_RANK, N_HEADS * LANES), l),
            _layer_spec((C_KV_RANK, N_HEADS * LANES), l),
            _layer_spec((C_KV_RANK, BRANCH_W), l),
            _const_spec((LANES, LANES)),
            _const_spec((tm, tm)),
            _layer_spec((1, C_Q_RANK), l),
            _layer_spec((1, C_KV_RANK), l),
            _layer_spec((1, LANES), l),
            pl.BlockSpec((6, tm, LANES), lambda i: (0, i % period, 0)),
        ],
        out_specs=[
            pl.BlockSpec((4 * N_HEADS, tm, LANES), lambda i: (0, i, 0)),
            pl.BlockSpec((2 * BRANCH_W, tm), lambda i: (0, i)),
            pl.BlockSpec((N_HEADS, tm, LANES), lambda i: (0, i, 0)),
            pl.BlockSpec((tm, LANES), lambda i: (i, 0)),
            pl.BlockSpec((LANES, tm), lambda i: (0, i)),
        ],
        scratch_shapes=[pltpu.VMEM((8, LANES), F32)],
    )
    out_shape = [
        jax.ShapeDtypeStruct((4 * N_HEADS, m, LANES), BF16),
        jax.ShapeDtypeStruct((2 * BRANCH_W, m), BF16),
        jax.ShapeDtypeStruct((N_HEADS, m, LANES), BF16),
        jax.ShapeDtypeStruct((m, LANES), BF16),
        jax.ShapeDtypeStruct((LANES, m), BF16),
    ]
    return pl.pallas_call(
        functools.partial(_pre_kernel, tm=tm, seq=seq),
        grid_spec=grid_spec,
        out_shape=out_shape,
        compiler_params=pltpu.CompilerParams(
            dimension_semantics=("arbitrary",), vmem_limit_bytes=VMEM_LIMIT_BYTES),
        name="pre",
    )(x, lw["norm_g"], lw["w1"], lw["wuq"], lw["wuk"], lw["wuv"], consts["aug_e"], consts["tri"],
      lw["q_norm_g"], lw["kv_norm_g"], lw["b_f"], consts["rope_tab"])


def _swa_kernel(q_ref, k_ref, vt_ref, bias_ref, sink_ref, o_ref, ot_scr, *, seq):
    nb = seq // BLK
    half = N_HEADS // 2 * BLK
    sink = sink_ref[...]
    def scores(n):
        q = q_ref[:, n * BLK:(n + 1) * BLK, :].reshape(N_HEADS * BLK, LANES)
        return _dot_nt(k_ref[max(n - 1, 0) * BLK:(n + 1) * BLK, :], q)

    s_next = scores(0)
    for n in range(nb):
        k0 = max(n - 1, 0) * BLK
        k1 = (n + 1) * BLK
        if n == 0:
            bias = bias_ref[2, BLK:, :]
        else:
            bias = bias_ref[0 if n >= 2 else 1]
        s = jnp.minimum(s_next, -NEG) + bias
        if n + 1 < nb:
            s_next = scores(n + 1)
        m = jnp.maximum(jnp.max(s, axis=0, keepdims=True), sink)
        pb = jnp.exp2(s - m).astype(BF16)
        sink_p = jnp.exp2(sink - m)
        ones = jnp.ones((SUM_ROWS, k1 - k0), BF16)
        for kv in range(A_KV_HEADS):
            vt = jnp.concatenate([vt_ref[kv * HEAD_DIM:(kv + 1) * HEAD_DIM, k0:k1], ones], axis=0)
            pv = _dot(vt, pb[:, kv * half:(kv + 1) * half])
            pv = pv[:HEAD_DIM] / (pv[HEAD_DIM:HEAD_DIM + 1] + sink_p[:, kv * half:(kv + 1) * half])
            for g in range(N_HEADS // A_KV_HEADS):
                h = kv * (N_HEADS // A_KV_HEADS) + g
                ot_scr[h * HEAD_DIM:(h + 1) * HEAD_DIM, n * BLK:(n + 1) * BLK] = pv[:, g * BLK:(g + 1) * BLK]
    for c in range(nb):
        o_ref[c * BLK:(c + 1) * BLK, :] = ot_scr[:, c * BLK:(c + 1) * BLK].T.astype(o_ref.dtype)


def _swa_call(qa, ka, vta, bias, sink, l, *, batch, seq):
    m = ka.shape[0]
    grid_spec = pl.GridSpec(
        grid=(batch,),
        in_specs=[
            pl.BlockSpec((N_HEADS, seq, LANES), lambda b: (0, b, 0)),
            pl.BlockSpec((seq, LANES), lambda b: (b, 0)),
            pl.BlockSpec((LANES, seq), lambda b: (0, b)),
            _const_spec((3, 2 * BLK, N_HEADS * BLK)),
            _layer_spec((1, N_HEADS * BLK), l),
        ],
        out_specs=pl.BlockSpec((seq, BRANCH_W), lambda b: (b, 0)),
        scratch_shapes=[pltpu.VMEM((BRANCH_W, seq), F32)],
    )
    return pl.pallas_call(
        functools.partial(_swa_kernel, seq=seq),
        grid_spec=grid_spec,
        out_shape=jax.ShapeDtypeStruct((m, BRANCH_W), BF16),
        compiler_params=pltpu.CompilerParams(
            dimension_semantics=("parallel",), vmem_limit_bytes=VMEM_LIMIT_BYTES),
        name="swa",
    )(qa, ka, vta, bias, sink)


def _dense_kernel(q_ref, k_ref, vt_ref, dcap_ref, mqcap_ref, o_ref, ot_scr, *, seq):
    def head_chains(c):
        groups = [[(c, 0, BLK, [([(0, BLK)], [(0, BLK)], [(mqcap_ref, 0, BLK)], 0)])]]
        for u in range((seq - BLK) // Q_SUB):
            q0 = BLK + Q_SUB * u
            n_full = u * Q_SUB // K_TILE
            steps = [([(BLK + K_TILE * j, K_TILE)],) * 2 + (None, 0) for j in range(n_full)]
            rows = q0 + Q_SUB - (BLK + K_TILE * n_full)
            diag = (BLK + K_TILE * n_full, rows)
            steps.append(([(PAD, N_META), diag], [(0, BLK), diag],
                          [(dcap_ref, PAD, N_META), (dcap_ref, BLK + K_TILE - rows, rows)], PAD))
            if u % CHAINS_PER_GROUP == 0:
                groups.append([])
            groups[-1].append((c, q0, Q_SUB, steps))
        return groups

    def gather(ref_rows, ranges, axis):
        parts = [ref_rows(a, n) for a, n in ranges]
        return parts[0] if len(parts) == 1 else jnp.concatenate(parts, axis=axis)

    def head_body(hg, carry):
        heads = [hg * HEADS_PER_ITER + c for c in range(HEADS_PER_ITER)]
        rows = [pl.ds(pl.multiple_of(h * HEAD_DIM, HEAD_DIM), HEAD_DIM) for h in heads]
        per_head = [head_chains(c) for c in range(HEADS_PER_ITER)]
        slots = []
        for gi in range(len(per_head[0])):
            group = [ch for groups in per_head for ch in groups[gi]]
            for t in range(len(group[0][3])):
                slots += [(ch, t) for ch in group]

        def scores(slot):
            (c, q0, qn, steps), t = slot
            keys = gather(lambda a, n: k_ref[heads[c], a:a + n, :], steps[t][0], 0)
            return _dot_nt(keys, q_ref[heads[c], q0:q0 + qn, :])

        ahead = [scores(sl) for sl in slots[:LOOKAHEAD]]
        state = {}
        for n, ((c, q0, qn, steps), t) in enumerate(slots):
            s = ahead.pop(0)
            if n + LOOKAHEAD < len(slots):
                ahead.append(scores(slots[n + LOOKAHEAD]))
            _, value_ranges, cap_ranges, zero_rows = steps[t]
            if cap_ranges is not None:
                cap = [ref[a:a + n, :] for ref, a, n in cap_ranges]
                s = jnp.minimum(s, cap[0] if len(cap) == 1 else jnp.concatenate(cap, axis=0))
            mc = jnp.max(s, axis=0, keepdims=True)
            prev = state.get((c, q0))
            m_new = mc if prev is None else jnp.maximum(prev[0], mc)
            pt = jnp.exp2(s - m_new).astype(BF16)
            if zero_rows:
                pt = jnp.concatenate([jnp.zeros((zero_rows, qn), BF16), pt], axis=0)
            vt = gather(lambda a, n: vt_ref[rows[c], a:a + n], value_ranges, 1)
            vt = jnp.concatenate([vt, jnp.ones((SUM_ROWS, vt.shape[1]), BF16)], axis=0)
            pv = _dot(vt, pt)
            acc = pv if prev is None else jnp.exp2(prev[0] - m_new) * prev[1] + pv
            state[c, q0] = (m_new, acc)
            if t == len(steps) - 1:
                del state[c, q0]
                ot_scr[rows[c], q0:q0 + qn] = acc[:HEAD_DIM] / acc[HEAD_DIM:HEAD_DIM + 1]
        return carry

    lax.fori_loop(0, N_HEADS // HEADS_PER_ITER, head_body, 0)
    for c in range(seq // BLK):
        o_ref[0, c * BLK:(c + 1) * BLK, :] = ot_scr[:, c * BLK:(c + 1) * BLK].T.astype(o_ref.dtype)


def _dense_call(qk, vt, consts, *, batch, seq):
    m = qk.shape[1]
    grid_spec = pl.GridSpec(
        grid=(batch, 2),
        in_specs=[
            pl.BlockSpec((N_HEADS, seq, LANES), lambda b, x: (x, b, 0)),
            pl.BlockSpec((N_HEADS, seq, LANES), lambda b, x: (2 + x, b, 0)),
            pl.BlockSpec((BRANCH_W, seq), lambda b, x: (x, b)),
            _const_spec((BLK + K_TILE, Q_SUB)),
            _const_spec((BLK, BLK)),
        ],
        out_specs=pl.BlockSpec((1, seq, BRANCH_W), lambda b, x: (x, b, 0)),
        scratch_shapes=[pltpu.VMEM((BRANCH_W, seq), F32)],
    )
    return pl.pallas_call(
        functools.partial(_dense_kernel, seq=seq),
        grid_spec=grid_spec,
        out_shape=jax.ShapeDtypeStruct((2, m, BRANCH_W), BF16),
        compiler_params=pltpu.CompilerParams(
            dimension_semantics=("parallel", "parallel"), vmem_limit_bytes=VMEM_LIMIT_BYTES),
        name="dense",
    )(qk, qk, vt, consts["dcap"], consts["mqcap"])


def _post_kernel(x_ref, oa_ref, obc_ref, g_ref, wz_ref, wg_ref, wbr_ref, wout_ref, fg_ref, out_ref, *, final):
    x = x_ref[...]
    hb = _rms(x, g_ref[...]).astype(BF16)
    y = None
    for i in range(N_BRANCH):
        o = (oa_ref[...] if i == 0 else obc_ref[i - 1]).astype(F32)
        z = _dot(hb, wz_ref[:, BRANCH_W * i:BRANCH_W * (i + 1)])
        gate = jax.nn.sigmoid(_dot(hb, wg_ref[:, D_MODEL * i:D_MODEL * (i + 1)]))
        u = (o * (z * jax.nn.sigmoid(z))).astype(BF16)
        t = _dot(u, wbr_ref[i])
        y = gate * t if y is None else y + gate * t
    out = x + _dot(y.astype(BF16), wout_ref[...])
    if final:
        out = _rms(out, fg_ref[...])
    out_ref[...] = out


def _post_call(x, oa, obc, lw, l, final_g, *, final, batch, seq):
    m = x.shape[0]
    tm = POST_TM
    if final:
        n_t = (seq - BLK) // tm
        grid = (batch, n_t)
        m_out = batch * (seq - BLK)

        def row0(b, t):
            return pl.multiple_of(b * seq + BLK + t * tm, BLK)

        el = pl.Element
        stream_specs = [
            pl.BlockSpec((el(tm), el(D_MODEL)), lambda b, t: (row0(b, t), 0)),
            pl.BlockSpec((el(tm), el(BRANCH_W)), lambda b, t: (row0(b, t), 0)),
            pl.BlockSpec((el(2), el(tm), el(BRANCH_W)), lambda b, t: (0, row0(b, t), 0)),
        ]
        out_spec = pl.BlockSpec((tm, D_MODEL), lambda b, t: (b * n_t + t, 0))
    else:
        grid = (m // tm,)
        m_out = m
        stream_specs = [
            pl.BlockSpec((tm, D_MODEL), lambda i: (i, 0)),
            pl.BlockSpec((tm, BRANCH_W), lambda i: (i, 0)),
            pl.BlockSpec((2, tm, BRANCH_W), lambda i: (0, i, 0)),
        ]
        out_spec = pl.BlockSpec((tm, D_MODEL), lambda i: (i, 0))
    grid_spec = pl.GridSpec(
        grid=grid,
        in_specs=stream_specs + [
            _layer_spec((1, D_MODEL), l),
            _layer_spec((D_MODEL, N_BRANCH * BRANCH_W), l),
            _layer_spec((D_MODEL, N_BRANCH * D_MODEL), l),
            _layer_spec((N_BRANCH, BRANCH_W, D_MODEL), l),
            _layer_spec((D_MODEL, D_MODEL), l),
            _const_spec((1, D_MODEL)),
        ],
        out_specs=out_spec,
    )
    return pl.pallas_call(
        functools.partial(_post_kernel, final=final),
        grid_spec=grid_spec,
        out_shape=jax.ShapeDtypeStruct((m_out, D_MODEL), F32),
        input_output_aliases={} if final else {0: 0},
        compiler_params=pltpu.CompilerParams(
            dimension_semantics=("parallel",) * len(grid), vmem_limit_bytes=VMEM_LIMIT_BYTES),
        name="post",
    )(x, oa, obc, lw["norm_g"], lw["wz"], lw["wg"], lw["w_br"], lw["w_out"], final_g)


def _rope_tables(seq, tm):
    pos = (jnp.arange(seq) - PAD).astype(F32)
    inv = ROPE_THETA ** (-jnp.arange(ROPE_HALF, dtype=F32) / ROPE_HALF)
    ang = pos[:, None] * inv[None, :]
    cos, sin = jnp.cos(ang), jnp.sin(ang)
    zero = jnp.zeros_like(cos)

    def lanes(nope, a, b):
        return jnp.concatenate([jnp.full((seq, C_NOPE), nope, F32), a, b,
                                jnp.zeros((seq, LANES - C_NOPE - C_ROPE), F32)], axis=1)

    tabs = jnp.stack([lanes(1.0, cos, cos) * SCALE_C, lanes(0.0, -sin, zero) * SCALE_C,
                      lanes(0.0, zero, sin) * SCALE_C,
                      lanes(0.0, cos, cos), lanes(0.0, -sin, zero), lanes(0.0, zero, sin)])
    return jnp.tile(tabs, (1, tm // LANES, 1))


def _aug_matrix():
    e = np.zeros((LANES, LANES), np.float32)
    for h in range(N_HEADS):
        q0, k0 = AUG_W * h, AUG_K0 + AUG_W * h
        for p in range(3):
            e[p * N_HEADS + h, q0 + p] = 1.0
            e[AUG_ONE, q0 + 3 + p] = 1.0
            e[AUG_ONE, k0 + p] = 1.0
            e[p * N_HEADS + h, k0 + 3 + p] = -1.0
    return jnp.asarray(e, BF16)


def _dense_caps():
    meta_ok = (np.arange(BLK)[:, None] >= PAD) & np.ones((1, Q_SUB), bool)
    kk = np.arange(K_TILE)[:, None] - (K_TILE - Q_SUB)
    diag_ok = kk <= np.arange(Q_SUB)[None, :]
    dcap = np.where(np.concatenate([meta_ok, diag_ok]), -NEG, NEG)
    km = np.arange(BLK)[:, None]
    mqcap = np.where((km >= PAD) & (km <= np.arange(BLK)[None, :]), -NEG, NEG)
    return jnp.asarray(dcap, F32), jnp.asarray(mqcap, F32)


def _swa_biases():
    slopes = 2.0 ** (-8.0 * (jnp.arange(N_HEADS, dtype=F32) + 1.0) / N_HEADS)
    s = np.arange(2 * BLK)[:, None]
    t = np.arange(BLK)[None, :]
    dist = BLK + t - s
    window = (dist >= 0) & (dist < BLK)
    tabs = []
    for mask in (window, window & (s >= PAD), window & (s >= BLK + PAD)):
        per_head = [jnp.where(mask, -slopes[h] * dist.astype(np.float32) * LOG2E, NEG) for h in range(N_HEADS)]
        tabs.append(jnp.concatenate(per_head, axis=1))
    return jnp.stack(tabs).astype(F32)


def _stacked_weights(norm_g, w_in, b_f, sinks, q_norm_g, kv_norm_g, w_uq, w_ukv, w_br, w_out):
    depth = w_in.shape[0]
    splits = np.cumsum([0, 512, 128, 128, 512, 512, 512, 8, 256, 128, 32, 1536, 3072])
    aq, ak, av, bq, bk, bv, bfl, cq, ckv, ckr, z, g = [w_in[:, :, splits[i]:splits[i + 1]] for i in range(12)]
    aq = aq.reshape(depth, D_MODEL, A_KV_HEADS, N_HEADS // A_KV_HEADS, HEAD_DIM)
    aq = jnp.transpose(aq, (0, 1, 3, 2, 4)).reshape(depth, D_MODEL, N_HEADS * HEAD_DIM)
    misc = jnp.concatenate([bfl, jnp.zeros((depth, D_MODEL, MISC_ROPE - N_HEADS), F32), ckr,
                            jnp.zeros((depth, D_MODEL, LANES - MISC_ROPE - C_ROPE), F32)], axis=2)
    w1 = jnp.concatenate([aq, ak, av, bq, bk, bv, cq, ckv, misc], axis=2).astype(BF16)
    wuq = jnp.pad(w_uq.reshape(depth, C_Q_RANK, N_HEADS, C_NOPE + C_ROPE),
                  ((0, 0), (0, 0), (0, 0), (0, LANES - C_NOPE - C_ROPE))).reshape(depth, C_Q_RANK, N_HEADS * LANES)
    wukv = w_ukv.reshape(depth, C_KV_RANK, N_HEADS, 2 * HEAD_DIM)
    wuk = jnp.pad(wukv[..., :C_NOPE], ((0, 0), (0, 0), (0, 0), (0, LANES - C_NOPE)))
    wuk = wuk.reshape(depth, C_KV_RANK, N_HEADS * LANES)
    wuv = wukv[..., C_NOPE:].reshape(depth, C_KV_RANK, BRANCH_W)
    return {
        "norm_g": norm_g.reshape(depth, 1, D_MODEL), "w1": w1, "wuq": wuq.astype(BF16), "wuk": wuk.astype(BF16),
        "wuv": wuv.astype(BF16), "q_norm_g": q_norm_g.reshape(depth, 1, C_Q_RANK),
        "kv_norm_g": kv_norm_g.reshape(depth, 1, C_KV_RANK),
        "b_f": jnp.pad(b_f, ((0, 0), (0, LANES - N_HEADS))).reshape(depth, 1, LANES),
        "sink": jnp.repeat(sinks * LOG2E, BLK, axis=1).reshape(depth, 1, N_HEADS * BLK),
        "wz": z.astype(BF16), "wg": g.astype(BF16), "w_br": w_br.astype(BF16), "w_out": w_out.astype(BF16),
    }


def kernel(x, meta_tokens, norm_g, w_in, b_f, sinks, q_norm_g, kv_norm_g, w_uq, w_ukv, w_br, w_out, final_norm_g):
    batch, n_real, _ = x.shape
    seq = BLK + n_real
    depth = w_in.shape[0]
    assert (seq - BLK) % (Q_SUB * CHAINS_PER_GROUP) == 0 and K_TILE % Q_SUB == 0 and (batch * seq) % PRE_TM == 0 and (batch * seq) % POST_TM == 0
    h = jnp.concatenate([jnp.zeros((batch, PAD, D_MODEL), x.dtype),
                         jnp.broadcast_to(meta_tokens[None].astype(x.dtype), (batch, N_META, D_MODEL)),
                         x], axis=1).reshape(batch * seq, D_MODEL)
    dcap, mqcap = _dense_caps()
    consts = {"aug_e": _aug_matrix(), "tri": jnp.asarray(np.tri(PRE_TM), BF16),
              "rope_tab": _rope_tables(seq, PRE_TM), "dcap": dcap, "mqcap": mqcap}
    swa_bias = _swa_biases()
    final_g = final_norm_g.reshape(1, D_MODEL)
    lw = _stacked_weights(norm_g, w_in, b_f, sinks, q_norm_g, kv_norm_g, w_uq, w_ukv, w_br, w_out)
    for l in range(depth):
        qk, vt, qa, ka, vta = _pre_call(h, lw, l, consts, seq=seq)
        oa = _swa_call(qa, ka, vta, swa_bias, lw["sink"], l, batch=batch, seq=seq)
        obc = _dense_call(qk, vt, consts, batch=batch, seq=seq)
        h = _post_call(h, oa, obc, lw, l, final_g, final=(l == depth - 1), batch=batch, seq=seq)
    return h.reshape(batch, n_real, D_MODEL)
```

```python
import functools
import math

import jax
import jax.numpy as jnp
import numpy as np
from jax import lax
from jax.experimental import pallas as pl
from jax.experimental.pallas import tpu as pltpu

D_MODEL = 1024
N_META = 16
BLK = 128
PAD = BLK - N_META
HEAD_DIM = 64
N_HEADS = 8
A_KV_HEADS = 2
C_Q_RANK = 256
C_KV_RANK = 128
C_NOPE = 64
C_ROPE = 32
ROPE_HALF = C_ROPE // 2
ROPE_THETA = 10000.0
N_BRANCH = 3
BRANCH_W = 512
EPS = 1e-6
NEG = -1e30
LOG2E = math.log2(math.e)

LANES = 128
Q_SUB = 512
K_TILE = 512
CHAINS_PER_GROUP = 1
LOOKAHEAD = 3
HEADS_PER_ITER = 4
SUM_ROWS = 16
PRE_TM = 512
POST_TM = 1024
VMEM_LIMIT_BYTES = 56 * 1024 * 1024

SCALE_AB = HEAD_DIM ** -0.5 * LOG2E
SCALE_C = (C_NOPE + C_ROPE) ** -0.5 * LOG2E

COL_AQ = 0
COL_AK = 512
COL_AV = 640
COL_BQ = 768
COL_BK = 1280
COL_BV = 1792
COL_CQ = 2304
COL_CKV = 2560
COL_MISC = 2688
N_PRE = 2816
MISC_ROPE = 64
AUG_ONE = 24
AUG_W = 6
AUG_K0 = N_HEADS * AUG_W

BF16 = jnp.bfloat16
F32 = jnp.float32


def _dot(a, b):
    return jnp.dot(a, b, preferred_element_type=F32)


def _dot_nt(a, b):
    return lax.dot_general(a, b, (((1,), (1,)), ((), ())), preferred_element_type=F32)


def _rms(x, g):
    return x * lax.rsqrt(jnp.mean(x * x, axis=-1, keepdims=True) + EPS) * g


def _split3(x):
    hi = x.astype(BF16).astype(F32)
    r = x - hi
    mid = r.astype(BF16).astype(F32)
    lo = (r - mid).astype(BF16).astype(F32)
    return hi, mid, lo


def _pre_kernel(x_ref, g_ref, w1_ref, wuq_ref, wuk_ref, wuv_ref, e_ref, tri_ref, qng_ref, kvng_ref,
                bf_ref, tab_ref, qk_ref, vt_ref, qa_ref, ka_ref, vta_ref, carry_ref, *, tm, seq):
    i = pl.program_id(0)

    @pl.when(i == 0)
    def _():
        carry_ref[...] = jnp.zeros_like(carry_ref)

    hb = _rms(x_ref[...], g_ref[...]).astype(BF16)
    lane = lax.broadcasted_iota(jnp.int32, (tm, LANES), 1)
    row = lax.broadcasted_iota(jnp.int32, (tm, LANES), 0)
    lo_half = lane < HEAD_DIM

    misc = _dot(hb, w1_ref[:, COL_MISC:N_PRE])
    rc = _dot(hb, w1_ref[:, COL_CQ:COL_MISC])
    ra = _dot(hb, w1_ref[:, COL_AQ:COL_BQ])

    for j in range(N_HEADS // 2):
        pq = ra[:, LANES * j:LANES * (j + 1)] * SCALE_AB
        qa_ref[j] = jnp.where(lo_half, pq, 0.0).astype(BF16)
        qa_ref[j + N_HEADS // 2] = jnp.where(lo_half, 0.0, pq).astype(BF16)
    ka_ref[...] = ra[:, COL_AK:COL_AV].astype(BF16)
    vta_ref[...] = ra[:, COL_AV:COL_BQ].T.astype(BF16)

    p0 = lax.rem(i * tm, seq)
    pib = p0 + row
    pib = jnp.where(pib >= seq, pib - seq, pib)
    xg = misc + bf_ref[...]
    logf = (jnp.minimum(xg, 0.0) - jnp.log1p(jnp.exp(-jnp.abs(xg)))) * LOG2E
    logf = jnp.where((pib >= PAD) & (lane < N_HEADS), logf, 0.0)

    def pieces(v):
        rep = v + pltpu.roll(v, N_HEADS, 1) + pltpu.roll(v, 2 * N_HEADS, 1)
        hi, mid, lo = _split3(rep)
        return jnp.where(lane < N_HEADS, hi, jnp.where(lane < 2 * N_HEADS, mid, lo))

    def gather_pieces(v):
        return v + pltpu.roll(v, LANES - N_HEADS, 1) + pltpu.roll(v, LANES - 2 * N_HEADS, 1)

    bnd = seq - p0
    cum = gather_pieces(_dot(tri_ref[...], pieces(logf).astype(BF16)))
    before = jnp.sum(jnp.where(row < bnd, logf, 0.0), axis=0, keepdims=True)

    def rope(v, c, s1, s2):
        return v * c + pltpu.roll(v, LANES - ROPE_HALF, 1) * s1 + pltpu.roll(v, ROPE_HALF, 1) * s2

    cqn = _rms(rc[:, :C_Q_RANK], qng_ref[...]).astype(BF16)
    ckvn = _rms(rc[:, C_Q_RANK:], kvng_ref[...]).astype(BF16)
    qc = _dot(cqn, wuq_ref[...])
    for h in range(N_HEADS):
        blk = qc[:, LANES * h:LANES * (h + 1)]
        qk_ref[N_HEADS + h] = rope(blk, tab_ref[0], tab_ref[1], tab_ref[2]).astype(BF16)

    kc = _dot(ckvn, wuk_ref[...])
    kr = rope(misc, tab_ref[3], tab_ref[4], tab_ref[5])
    for h in range(N_HEADS):
        qk_ref[3 * N_HEADS + h] = jnp.where(lo_half, kc[:, LANES * h:LANES * (h + 1)], kr).astype(BF16)
    vt_ref[BRANCH_W:2 * BRANCH_W, :] = _dot(ckvn, wuv_ref[...]).T.astype(BF16)

    vt_ref[0:BRANCH_W, :] = _dot(hb, w1_ref[:, COL_BV:COL_CQ]).T.astype(BF16)

    carry_in = jnp.where(p0 == 0, 0.0, carry_ref[0:1, :])
    cum = jnp.where(lane < N_HEADS, cum + jnp.where(row >= bnd, -before, carry_in), 0.0)
    carry_ref[...] = jnp.broadcast_to(cum[tm - 1:tm, :], carry_ref.shape)
    pc = jnp.where(lane == AUG_ONE, 1.0, pieces(cum)).astype(BF16)
    aug = _dot(pc, e_ref[...])

    rb = _dot(hb, w1_ref[:, COL_BQ:COL_BV])
    for h in range(N_HEADS):
        j, t = divmod(h, 2)
        own = lo_half if t == 0 else jnp.logical_not(lo_half)
        a = HEAD_DIM if t == 0 else 0
        in_aug = (lane >= a) & (lane < a + AUG_W)
        pq = rb[:, LANES * j:LANES * (j + 1)] * SCALE_AB
        pk = rb[:, BRANCH_W + LANES * j:BRANCH_W + LANES * (j + 1)]
        aq = pltpu.roll(aug, (a - AUG_W * h) % LANES, 1)
        ak = pltpu.roll(aug, (a - AUG_K0 - AUG_W * h) % LANES, 1)
        qk_ref[h] = jnp.where(own, pq, jnp.where(in_aug, aq, 0.0)).astype(BF16)
        qk_ref[2 * N_HEADS + h] = jnp.where(own, pk, jnp.where(in_aug, ak, 0.0)).astype(BF16)


def _const_spec(shape):
    nd = len(shape)
    return pl.BlockSpec(shape, lambda *_: (0,) * nd, pipeline_mode=pl.Buffered(1))


def _layer_spec(shape, l):
    nd = len(shape)
    return pl.BlockSpec((None,) + shape, lambda *_: (l,) + (0,) * nd, pipeline_mode=pl.Buffered(1))


def _pre_call(x, lw, l, consts, *, seq):
    m = x.shape[0]
    tm = PRE_TM
    n_tiles = m // tm
    period = seq // LANES
    grid_spec = pl.GridSpec(
        grid=(n_tiles,),
        in_specs=[
            pl.BlockSpec((tm, D_MODEL), lambda i: (i, 0)),
            _layer_spec((1, D_MODEL), l),
            _layer_spec((D_MODEL, N_PRE), l),
            _layer_spec((C_Q_RANK, N_HEADS * LANES), l),
            _layer_spec((C_KV_RANK, N_HEADS * LANES), l),
            _layer_spec((C_KV_RANK, BRANCH_W), l),
            _const_spec((LANES, LANES)),
            _const_spec((tm, tm)),
            _layer_spec((1, C_Q_RANK), l),
            _layer_spec((1, C_KV_RANK), l),
            _layer_spec((1, LANES), l),
            pl.BlockSpec((6, tm, LANES), lambda i: (0, i % period, 0)),
        ],
        out_specs=[
            pl.BlockSpec((4 * N_HEADS, tm, LANES), lambda i: (0, i, 0)),
            pl.BlockSpec((2 * BRANCH_W, tm), lambda i: (0, i)),
            pl.BlockSpec((N_HEADS, tm, LANES), lambda i: (0, i, 0)),
            pl.BlockSpec((tm, LANES), lambda i: (i, 0)),
            pl.BlockSpec((LANES, tm), lambda i: (0, i)),
        ],
        scratch_shapes=[pltpu.VMEM((8, LANES), F32)],
    )
    out_shape = [
        jax.ShapeDtypeStruct((4 * N_HEADS, m, LANES), BF16),
        jax.ShapeDtypeStruct((2 * BRANCH_W, m), BF16),
        jax.ShapeDtypeStruct((N_HEADS, m, LANES), BF16),
        jax.ShapeDtypeStruct((m, LANES), BF16),
        jax.ShapeDtypeStruct((LANES, m), BF16),
    ]
    return pl.pallas_call(
        functools.partial(_pre_kernel, tm=tm, seq=seq),
        grid_spec=grid_spec,
        out_shape=out_shape,
        compiler_params=pltpu.CompilerParams(
            dimension_semantics=("arbitrary",), vmem_limit_bytes=VMEM_LIMIT_BYTES),
        name="pre",
    )(x, lw["norm_g"], lw["w1"], lw["wuq"], lw["wuk"], lw["wuv"], consts["aug_e"], consts["tri"],
      lw["q_norm_g"], lw["kv_norm_g"], lw["b_f"], consts["rope_tab"])


def _swa_kernel(q_ref, k_ref, vt_ref, bias_ref, sink_ref, o_ref, ot_scr, *, seq):
    nb = seq // BLK
    half = N_HEADS // 2 * BLK
    sink = sink_ref[...]
    def scores(n):
        q = q_ref[:, n * BLK:(n + 1) * BLK, :].reshape(N_HEADS * BLK, LANES)
        return _dot_nt(k_ref[max(n - 1, 0) * BLK:(n + 1) * BLK, :], q)

    s_next = scores(0)
    for n in range(nb):
        k0 = max(n - 1, 0) * BLK
        k1 = (n + 1) * BLK
        if n == 0:
            bias = bias_ref[2, BLK:, :]
        else:
            bias = bias_ref[0 if n >= 2 else 1]
        s = jnp.minimum(s_next, -NEG) + bias
        if n + 1 < nb:
            s_next = scores(n + 1)
        m = jnp.maximum(jnp.max(s, axis=0, keepdims=True), sink)
        pb = jnp.exp2(s - m).astype(BF16)
        sink_p = jnp.exp2(sink - m)
        ones = jnp.ones((SUM_ROWS, k1 - k0), BF16)
        for kv in range(A_KV_HEADS):
            vt = jnp.concatenate([vt_ref[kv * HEAD_DIM:(kv + 1) * HEAD_DIM, k0:k1], ones], axis=0)
            pv = _dot(vt, pb[:, kv * half:(kv + 1) * half])
            pv = pv[:HEAD_DIM] / (pv[HEAD_DIM:HEAD_DIM + 1] + sink_p[:, kv * half:(kv + 1) * half])
            for g in range(N_HEADS // A_KV_HEADS):
                h = kv * (N_HEADS // A_KV_HEADS) + g
                ot_scr[h * HEAD_DIM:(h + 1) * HEAD_DIM, n * BLK:(n + 1) * BLK] = pv[:, g * BLK:(g + 1) * BLK]
    for c in range(nb):
        o_ref[c * BLK:(c + 1) * BLK, :] = ot_scr[:, c * BLK:(c + 1) * BLK].T.astype(o_ref.dtype)


def _swa_call(qa, ka, vta, bias, sink, l, *, batch, seq):
    m = ka.shape[0]
    grid_spec = pl.GridSpec(
        grid=(batch,),
        in_specs=[
            pl.BlockSpec((N_HEADS, seq, LANES), lambda b: (0, b, 0)),
            pl.BlockSpec((seq, LANES), lambda b: (b, 0)),
            pl.BlockSpec((LANES, seq), lambda b: (0, b)),
            _const_spec((3, 2 * BLK, N_HEADS * BLK)),
            _layer_spec((1, N_HEADS * BLK), l),
        ],
        out_specs=pl.BlockSpec((seq, BRANCH_W), lambda b: (b, 0)),
        scratch_shapes=[pltpu.VMEM((BRANCH_W, seq), F32)],
    )
    return pl.pallas_call(
        functools.partial(_swa_kernel, seq=seq),
        grid_spec=grid_spec,
        out_shape=jax.ShapeDtypeStruct((m, BRANCH_W), BF16),
        compiler_params=pltpu.CompilerParams(
            dimension_semantics=("parallel",), vmem_limit_bytes=VMEM_LIMIT_BYTES),
        name="swa",
    )(qa, ka, vta, bias, sink)


def _dense_kernel(q_ref, k_ref, vt_ref, dcap_ref, mqcap_ref, o_ref, ot_scr, *, seq):
    def head_chains(c):
        groups = [[(c, 0, BLK, [([(0, BLK)], [(0, BLK)], [(mqcap_ref, 0, BLK)], 0)])]]
        for u in range((seq - BLK) // Q_SUB):
            q0 = BLK + Q_SUB * u
            n_full = u * Q_SUB // K_TILE
            steps = [([(BLK + K_TILE * j, K_TILE)],) * 2 + (None, 0) for j in range(n_full)]
            rows = q0 + Q_SUB - (BLK + K_TILE * n_full)
            diag = (BLK + K_TILE * n_full, rows)
            steps.append(([(PAD, N_META), diag], [(0, BLK), diag],
                          [(dcap_ref, PAD, N_META), (dcap_ref, BLK + K_TILE - rows, rows)], PAD))
            if u % CHAINS_PER_GROUP == 0:
                groups.append([])
            groups[-1].append((c, q0, Q_SUB, steps))
        return groups

    def gather(ref_rows, ranges, axis):
        parts = [ref_rows(a, n) for a, n in ranges]
        return parts[0] if len(parts) == 1 else jnp.concatenate(parts, axis=axis)

    def head_body(hg, carry):
        heads = [hg * HEADS_PER_ITER + c for c in range(HEADS_PER_ITER)]
        rows = [pl.ds(pl.multiple_of(h * HEAD_DIM, HEAD_DIM), HEAD_DIM) for h in heads]
        per_head = [head_chains(c) for c in range(HEADS_PER_ITER)]
        slots = []
        for gi in range(len(per_head[0])):
            group = [ch for groups in per_head for ch in groups[gi]]
            for t in range(len(group[0][3])):
                slots += [(ch, t) for ch in group]

        def scores(slot):
            (c, q0, qn, steps), t = slot
            keys = gather(lambda a, n: k_ref[heads[c], a:a + n, :], steps[t][0], 0)
            return _dot_nt(keys, q_ref[heads[c], q0:q0 + qn, :])

        ahead = [scores(sl) for sl in slots[:LOOKAHEAD]]
        state = {}
        for n, ((c, q0, qn, steps), t) in enumerate(slots):
            s = ahead.pop(0)
            if n + LOOKAHEAD < len(slots):
                ahead.append(scores(slots[n + LOOKAHEAD]))
            _, value_ranges, cap_ranges, zero_rows = steps[t]
            if cap_ranges is not None:
                cap = [ref[a:a + n, :] for ref, a, n in cap_ranges]
                s = jnp.minimum(s, cap[0] if len(cap) == 1 else jnp.concatenate(cap, axis=0))
            mc = jnp.max(s, axis=0, keepdims=True)
            prev = state.get((c, q0))
            m_new = mc if prev is None else jnp.maximum(prev[0], mc)
            pt = jnp.exp2(s - m_new).astype(BF16)
            if zero_rows:
                pt = jnp.concatenate([jnp.zeros((zero_rows, qn), BF16), pt], axis=0)
            vt = gather(lambda a, n: vt_ref[rows[c], a:a + n], value_ranges, 1)
            vt = jnp.concatenate([vt, jnp.ones((SUM_ROWS, vt.shape[1]), BF16)], axis=0)
            pv = _dot(vt, pt)
            acc = pv if prev is None else jnp.exp2(prev[0] - m_new) * prev[1] + pv
            state[c, q0] = (m_new, acc)
            if t == len(steps) - 1:
                del state[c, q0]
                ot_scr[rows[c], q0:q0 + qn] = acc[:HEAD_DIM] / acc[HEAD_DIM:HEAD_DIM + 1]
        return carry

    lax.fori_loop(0, N_HEADS // HEADS_PER_ITER, head_body, 0)
    for c in range(seq // BLK):
        o_ref[0, c * BLK:(c + 1) * BLK, :] = ot_scr[:, c * BLK:(c + 1) * BLK].T.astype(o_ref.dtype)


def _dense_call(qk, vt, consts, *, batch, seq):
    m = qk.shape[1]
    grid_spec = pl.GridSpec(
        grid=(batch, 2),
        in_specs=[
            pl.BlockSpec((N_HEADS, seq, LANES), lambda b, x: (x, b, 0)),
            pl.BlockSpec((N_HEADS, seq, LANES), lambda b, x: (2 + x, b, 0)),
            pl.BlockSpec((BRANCH_W, seq), lambda b, x: (x, b)),
            _const_spec((BLK + K_TILE, Q_SUB)),
            _const_spec((BLK, BLK)),
        ],
        out_specs=pl.BlockSpec((1, seq, BRANCH_W), lambda b, x: (x, b, 0)),
        scratch_shapes=[pltpu.VMEM((BRANCH_W, seq), F32)],
    )
    return pl.pallas_call(
        functools.partial(_dense_kernel, seq=seq),
        grid_spec=grid_spec,
        out_shape=jax.ShapeDtypeStruct((2, m, BRANCH_W), BF16),
        compiler_params=pltpu.CompilerParams(
            dimension_semantics=("parallel", "parallel"), vmem_limit_bytes=VMEM_LIMIT_BYTES),
        name="dense",
    )(qk, qk, vt, consts["dcap"], consts["mqcap"])


def _post_kernel(x_ref, oa_ref, obc_ref, g_ref, wz_ref, wg_ref, wbr_ref, wout_ref, fg_ref, out_ref, *, final):
    x = x_ref[...]
    hb = _rms(x, g_ref[...]).astype(BF16)
    y = None
    for i in range(N_BRANCH):
        o = (oa_ref[...] if i == 0 else obc_ref[i - 1]).astype(F32)
        z = _dot(hb, wz_ref[:, BRANCH_W * i:BRANCH_W * (i + 1)])
        gate = jax.nn.sigmoid(_dot(hb, wg_ref[:, D_MODEL * i:D_MODEL * (i + 1)]))
        u = (o * (z * jax.nn.sigmoid(z))).astype(BF16)
        t = _dot(u, wbr_ref[i])
        y = gate * t if y is None else y + gate * t
    out = x + _dot(y.astype(BF16), wout_ref[...])
    if final:
        out = _rms(out, fg_ref[...])
    out_ref[...] = out


def _post_call(x, oa, obc, lw, l, final_g, *, final, batch, seq):
    m = x.shape[0]
    tm = POST_TM
    if final:
        n_t = (seq - BLK) // tm
        grid = (batch, n_t)
        m_out = batch * (seq - BLK)

        def row0(b, t):
            return pl.multiple_of(b * seq + BLK + t * tm, BLK)

        el = pl.Element
        stream_specs = [
            pl.BlockSpec((el(tm), el(D_MODEL)), lambda b, t: (row0(b, t), 0)),
            pl.BlockSpec((el(tm), el(BRANCH_W)), lambda b, t: (row0(b, t), 0)),
            pl.BlockSpec((el(2), el(tm), el(BRANCH_W)), lambda b, t: (0, row0(b, t), 0)),
        ]
        out_spec = pl.BlockSpec((tm, D_MODEL), lambda b, t: (b * n_t + t, 0))
    else:
        grid = (m // tm,)
        m_out = m
        stream_specs = [
            pl.BlockSpec((tm, D_MODEL), lambda i: (i, 0)),
            pl.BlockSpec((tm, BRANCH_W), lambda i: (i, 0)),
            pl.BlockSpec((2, tm, BRANCH_W), lambda i: (0, i, 0)),
        ]
        out_spec = pl.BlockSpec((tm, D_MODEL), lambda i: (i, 0))
    grid_spec = pl.GridSpec(
        grid=grid,
        in_specs=stream_specs + [
            _layer_spec((1, D_MODEL), l),
            _layer_spec((D_MODEL, N_BRANCH * BRANCH_W), l),
            _layer_spec((D_MODEL, N_BRANCH * D_MODEL), l),
            _layer_spec((N_BRANCH, BRANCH_W, D_MODEL), l),
            _layer_spec((D_MODEL, D_MODEL), l),
            _const_spec((1, D_MODEL)),
        ],
        out_specs=out_spec,
    )
    return pl.pallas_call(
        functools.partial(_post_kernel, final=final),
        grid_spec=grid_spec,
        out_shape=jax.ShapeDtypeStruct((m_out, D_MODEL), F32),
        input_output_aliases={} if final else {0: 0},
        compiler_params=pltpu.CompilerParams(
            dimension_semantics=("parallel",) * len(grid), vmem_limit_bytes=VMEM_LIMIT_BYTES),
        name="post",
    )(x, oa, obc, lw["norm_g"], lw["wz"], lw["wg"], lw["w_br"], lw["w_out"], final_g)


def _rope_tables(seq, tm):
    pos = (jnp.arange(seq) - PAD).astype(F32)
    inv = ROPE_THETA ** (-jnp.arange(ROPE_HALF, dtype=F32) / ROPE_HALF)
    ang = pos[:, None] * inv[None, :]
    cos, sin = jnp.cos(ang), jnp.sin(ang)
    zero = jnp.zeros_like(cos)

    def lanes(nope, a, b):
        return jnp.concatenate([jnp.full((seq, C_NOPE), nope, F32), a, b,
                                jnp.zeros((seq, LANES - C_NOPE - C_ROPE), F32)], axis=1)

    tabs = jnp.stack([lanes(1.0, cos, cos) * SCALE_C, lanes(0.0, -sin, zero) * SCALE_C,
                      lanes(0.0, zero, sin) * SCALE_C,
                      lanes(0.0, cos, cos), lanes(0.0, -sin, zero), lanes(0.0, zero, sin)])
    return jnp.tile(tabs, (1, tm // LANES, 1))


def _aug_matrix():
    e = np.zeros((LANES, LANES), np.float32)
    for h in range(N_HEADS):
        q0, k0 = AUG_W * h, AUG_K0 + AUG_W * h
        for p in range(3):
            e[p * N_HEADS + h, q0 + p] = 1.0
            e[AUG_ONE, q0 + 3 + p] = 1.0
            e[AUG_ONE, k0 + p] = 1.0
            e[p * N_HEADS + h, k0 + 3 + p] = -1.0
    return jnp.asarray(e, BF16)


def _dense_caps():
    meta_ok = (np.arange(BLK)[:, None] >= PAD) & np.ones((1, Q_SUB), bool)
    kk = np.arange(K_TILE)[:, None] - (K_TILE - Q_SUB)
    diag_ok = kk <= np.arange(Q_SUB)[None, :]
    dcap = np.where(np.concatenate([meta_ok, diag_ok]), -NEG, NEG)
    km = np.arange(BLK)[:, None]
    mqcap = np.where((km >= PAD) & (km <= np.arange(BLK)[None, :]), -NEG, NEG)
    return jnp.asarray(dcap, F32), jnp.asarray(mqcap, F32)


def _swa_biases():
    slopes = 2.0 ** (-8.0 * (jnp.arange(N_HEADS, dtype=F32) + 1.0) / N_HEADS)
    s = np.arange(2 * BLK)[:, None]
    t = np.arange(BLK)[None, :]
    dist = BLK + t - s
    window = (dist >= 0) & (dist < BLK)
    tabs = []
    for mask in (window, window & (s >= PAD), window & (s >= BLK + PAD)):
        per_head = [jnp.where(mask, -slopes[h] * dist.astype(np.float32) * LOG2E, NEG) for h in range(N_HEADS)]
        tabs.append(jnp.concatenate(per_head, axis=1))
    return jnp.stack(tabs).astype(F32)


def _stacked_weights(norm_g, w_in, b_f, sinks, q_norm_g, kv_norm_g, w_uq, w_ukv, w_br, w_out):
    depth = w_in.shape[0]
    splits = np.cumsum([0, 512, 128, 128, 512, 512, 512, 8, 256, 128, 32, 1536, 3072])
    aq, ak, av, bq, bk, bv, bfl, cq, ckv, ckr, z, g = [w_in[:, :, splits[i]:splits[i + 1]] for i in range(12)]
    aq = aq.reshape(depth, D_MODEL, A_KV_HEADS, N_HEADS // A_KV_HEADS, HEAD_DIM)
    aq = jnp.transpose(aq, (0, 1, 3, 2, 4)).reshape(depth, D_MODEL, N_HEADS * HEAD_DIM)
    misc = jnp.concatenate([bfl, jnp.zeros((depth, D_MODEL, MISC_ROPE - N_HEADS), F32), ckr,
                            jnp.zeros((depth, D_MODEL, LANES - MISC_ROPE - C_ROPE), F32)], axis=2)
    w1 = jnp.concatenate([aq, ak, av, bq, bk, bv, cq, ckv, misc], axis=2).astype(BF16)
    wuq = jnp.pad(w_uq.reshape(depth, C_Q_RANK, N_HEADS, C_NOPE + C_ROPE),
                  ((0, 0), (0, 0), (0, 0), (0, LANES - C_NOPE - C_ROPE))).reshape(depth, C_Q_RANK, N_HEADS * LANES)
    wukv = w_ukv.reshape(depth, C_KV_RANK, N_HEADS, 2 * HEAD_DIM)
    wuk = jnp.pad(wukv[..., :C_NOPE], ((0, 0), (0, 0), (0, 0), (0, LANES - C_NOPE)))
    wuk = wuk.reshape(depth, C_KV_RANK, N_HEADS * LANES)
    wuv = wukv[..., C_NOPE:].reshape(depth, C_KV_RANK, BRANCH_W)
    return {
        "norm_g": norm_g.reshape(depth, 1, D_MODEL), "w1": w1, "wuq": wuq.astype(BF16), "wuk": wuk.astype(BF16),
        "wuv": wuv.astype(BF16), "q_norm_g": q_norm_g.reshape(depth, 1, C_Q_RANK),
        "kv_norm_g": kv_norm_g.reshape(depth, 1, C_KV_RANK),
        "b_f": jnp.pad(b_f, ((0, 0), (0, LANES - N_HEADS))).reshape(depth, 1, LANES),
        "sink": jnp.repeat(sinks * LOG2E, BLK, axis=1).reshape(depth, 1, N_HEADS * BLK),
        "wz": z.astype(BF16), "wg": g.astype(BF16), "w_br": w_br.astype(BF16), "w_out": w_out.astype(BF16),
    }


def kernel(x, meta_tokens, norm_g, w_in, b_f, sinks, q_norm_g, kv_norm_g, w_uq, w_ukv, w_br, w_out, final_norm_g):
    batch, n_real, _ = x.shape
    seq = BLK + n_real
    depth = w_in.shape[0]
    assert (seq - BLK) % (Q_SUB * CHAINS_PER_GROUP) == 0 and K_TILE % Q_SUB == 0 and (batch * seq) % PRE_TM == 0 and (batch * seq) % POST_TM == 0
    h = jnp.concatenate([jnp.zeros((batch, PAD, D_MODEL), x.dtype),
                         jnp.broadcast_to(meta_tokens[None].astype(x.dtype), (batch, N_META, D_MODEL)),
                         x], axis=1).reshape(batch * seq, D_MODEL)
    dcap, mqcap = _dense_caps()
    consts = {"aug_e": _aug_matrix(), "tri": jnp.asarray(np.tri(PRE_TM), BF16),
              "rope_tab": _rope_tables(seq, PRE_TM), "dcap": dcap, "mqcap": mqcap}
    swa_bias = _swa_biases()
    final_g = final_norm_g.reshape(1, D_MODEL)
    lw = _stacked_weights(norm_g, w_in, b_f, sinks, q_norm_g, kv_norm_g, w_uq, w_ukv, w_br, w_out)
    for l in range(depth):
        qk, vt, qa, ka, vta = _pre_call(h, lw, l, consts, seq=seq)
        oa = _swa_call(qa, ka, vta, swa_bias, lw["sink"], l, batch=batch, seq=seq)
        obc = _dense_call(qk, vt, consts, batch=batch, seq=seq)
        h = _post_call(h, oa, obc, lw, l, final_g, final=(l == depth - 1), batch=batch, seq=seq)
    return h.reshape(batch, n_real, D_MODEL)
```

```python
import functools
import math

import jax
import jax.numpy as jnp
import numpy as np
from jax import lax
from jax.experimental import pallas as pl
from jax.experimental.pallas import tpu as pltpu

D_MODEL = 1024
N_META = 16
BLK = 128
PAD = BLK - N_META
HEAD_DIM = 64
N_HEADS = 8
A_KV_HEADS = 2
C_Q_RANK = 256
C_KV_RANK = 128
C_NOPE = 64
C_ROPE = 32
ROPE_HALF = C_ROPE // 2
ROPE_THETA = 10000.0
N_BRANCH = 3
BRANCH_W = 512
EPS = 1e-6
NEG = -1e30
LOG2E = math.log2(math.e)

LANES = 128
Q_SUB = 256
K_TILE = 512
CHAINS_PER_GROUP = 2
LOOKAHEAD = 3
HEADS_PER_ITER = 4
SUM_ROWS = 16
PRE_TM = 512
POST_TM = 1024
VMEM_LIMIT_BYTES = 56 * 1024 * 1024

SCALE_AB = HEAD_DIM ** -0.5 * LOG2E
SCALE_C = (C_NOPE + C_ROPE) ** -0.5 * LOG2E

COL_AQ = 0
COL_AK = 512
COL_AV = 640
COL_BQ = 768
COL_BK = 1280
COL_BV = 1792
COL_CQ = 2304
COL_CKV = 2560
COL_MISC = 2688
N_PRE = 2816
MISC_ROPE = 64
AUG_ONE = 24
AUG_W = 6
AUG_K0 = N_HEADS * AUG_W

BF16 = jnp.bfloat16
F32 = jnp.float32


def _dot(a, b):
    return jnp.dot(a, b, preferred_element_type=F32)


def _dot_nt(a, b):
    return lax.dot_general(a, b, (((1,), (1,)), ((), ())), preferred_element_type=F32)


def _rms(x, g):
    return x * lax.rsqrt(jnp.mean(x * x, axis=-1, keepdims=True) + EPS) * g


def _split3(x):
    hi = x.astype(BF16).astype(F32)
    r = x - hi
    mid = r.astype(BF16).astype(F32)
    lo = (r - mid).astype(BF16).astype(F32)
    return hi, mid, lo


def _pre_kernel(x_ref, g_ref, w1_ref, wuq_ref, wuk_ref, wuv_ref, e_ref, tri_ref, qng_ref, kvng_ref,
                bf_ref, tab_ref, qk_ref, vt_ref, qa_ref, ka_ref, vta_ref, carry_ref, *, tm, seq):
    i = pl.program_id(0)

    @pl.when(i == 0)
    def _():
        carry_ref[...] = jnp.zeros_like(carry_ref)

    hb = _rms(x_ref[...], g_ref[...]).astype(BF16)
    lane = lax.broadcasted_iota(jnp.int32, (tm, LANES), 1)
    row = lax.broadcasted_iota(jnp.int32, (tm, LANES), 0)
    lo_half = lane < HEAD_DIM

    misc = _dot(hb, w1_ref[:, COL_MISC:N_PRE])
    rc = _dot(hb, w1_ref[:, COL_CQ:COL_MISC])
    ra = _dot(hb, w1_ref[:, COL_AQ:COL_BQ])

    for j in range(N_HEADS // 2):
        pq = ra[:, LANES * j:LANES * (j + 1)] * SCALE_AB
        qa_ref[j] = jnp.where(lo_half, pq, 0.0).astype(BF16)
        qa_ref[j + N_HEADS // 2] = jnp.where(lo_half, 0.0, pq).astype(BF16)
    ka_ref[...] = ra[:, COL_AK:COL_AV].astype(BF16)
    vta_ref[...] = ra[:, COL_AV:COL_BQ].T.astype(BF16)

    p0 = lax.rem(i * tm, seq)
    pib = p0 + row
    pib = jnp.where(pib >= seq, pib - seq, pib)
    xg = misc + bf_ref[...]
    logf = (jnp.minimum(xg, 0.0) - jnp.log1p(jnp.exp(-jnp.abs(xg)))) * LOG2E
    logf = jnp.where((pib >= PAD) & (lane < N_HEADS), logf, 0.0)

    def pieces(v):
        rep = v + pltpu.roll(v, N_HEADS, 1) + pltpu.roll(v, 2 * N_HEADS, 1)
        hi, mid, lo = _split3(rep)
        return jnp.where(lane < N_HEADS, hi, jnp.where(lane < 2 * N_HEADS, mid, lo))

    def gather_pieces(v):
        return v + pltpu.roll(v, LANES - N_HEADS, 1) + pltpu.roll(v, LANES - 2 * N_HEADS, 1)

    bnd = seq - p0
    cum = gather_pieces(_dot(tri_ref[...], pieces(logf).astype(BF16)))
    before = jnp.sum(jnp.where(row < bnd, logf, 0.0), axis=0, keepdims=True)

    def rope(v, c, s1, s2):
        return v * c + pltpu.roll(v, LANES - ROPE_HALF, 1) * s1 + pltpu.roll(v, ROPE_HALF, 1) * s2

    cqn = _rms(rc[:, :C_Q_RANK], qng_ref[...]).astype(BF16)
    ckvn = _rms(rc[:, C_Q_RANK:], kvng_ref[...]).astype(BF16)
    qc = _dot(cqn, wuq_ref[...])
    for h in range(N_HEADS):
        blk = qc[:, LANES * h:LANES * (h + 1)]
        qk_ref[N_HEADS + h] = rope(blk, tab_ref[0], tab_ref[1], tab_ref[2]).astype(BF16)

    kc = _dot(ckvn, wuk_ref[...])
    kr = rope(misc, tab_ref[3], tab_ref[4], tab_ref[5])
    for h in range(N_HEADS):
        qk_ref[3 * N_HEADS + h] = jnp.where(lo_half, kc[:, LANES * h:LANES * (h + 1)], kr).astype(BF16)
    vt_ref[BRANCH_W:2 * BRANCH_W, :] = _dot(ckvn, wuv_ref[...]).T.astype(BF16)

    vt_ref[0:BRANCH_W, :] = _dot(hb, w1_ref[:, COL_BV:COL_CQ]).T.astype(BF16)

    carry_in = jnp.where(p0 == 0, 0.0, carry_ref[0:1, :])
    cum = jnp.where(lane < N_HEADS, cum + jnp.where(row >= bnd, -before, carry_in), 0.0)
    carry_ref[...] = jnp.broadcast_to(cum[tm - 1:tm, :], carry_ref.shape)
    pc = jnp.where(lane == AUG_ONE, 1.0, pieces(cum)).astype(BF16)
    aug = _dot(pc, e_ref[...])

    rb = _dot(hb, w1_ref[:, COL_BQ:COL_BV])
    for h in range(N_HEADS):
        j, t = divmod(h, 2)
        own = lo_half if t == 0 else jnp.logical_not(lo_half)
        a = HEAD_DIM if t == 0 else 0
        in_aug = (lane >= a) & (lane < a + AUG_W)
        pq = rb[:, LANES * j:LANES * (j + 1)] * SCALE_AB
        pk = rb[:, BRANCH_W + LANES * j:BRANCH_W + LANES * (j + 1)]
        aq = pltpu.roll(aug, (a - AUG_W * h) % LANES, 1)
        ak = pltpu.roll(aug, (a - AUG_K0 - AUG_W * h) % LANES, 1)
        qk_ref[h] = jnp.where(own, pq, jnp.where(in_aug, aq, 0.0)).astype(BF16)
        qk_ref[2 * N_HEADS + h] = jnp.where(own, pk, jnp.where(in_aug, ak, 0.0)).astype(BF16)


def _const_spec(shape):
    nd = len(shape)
    return pl.BlockSpec(shape, lambda *_: (0,) * nd, pipeline_mode=pl.Buffered(1))


def _layer_spec(shape, l):
    nd = len(shape)
    return pl.BlockSpec((None,) + shape, lambda *_: (l,) + (0,) * nd, pipeline_mode=pl.Buffered(1))


def _pre_call(x, lw, l, consts, *, seq):
    m = x.shape[0]
    tm = PRE_TM
    n_tiles = m // tm
    period = seq // LANES
    grid_spec = pl.GridSpec(
        grid=(n_tiles,),
        in_specs=[
            pl.BlockSpec((tm, D_MODEL), lambda i: (i, 0)),
            _layer_spec((1, D_MODEL), l),
            _layer_spec((D_MODEL, N_PRE), l),
            _layer_spec((C_Q_RANK, N_HEADS * LANES), l),
            _layer_spec((C_KV_RANK, N_HEADS * LANES), l),
            _layer_spec((C_KV_RANK, BRANCH_W), l),
            _const_spec((LANES, LANES)),
            _const_spec((tm, tm)),
            _layer_spec((1, C_Q_RANK), l),
            _layer_spec((1, C_KV_RANK), l),
            _layer_spec((1, LANES), l),
            pl.BlockSpec((6, tm, LANES), lambda i: (0, i % period, 0)),
        ],
        out_specs=[
            pl.BlockSpec((4 * N_HEADS, tm, LANES), lambda i: (0, i, 0)),
            pl.BlockSpec((2 * BRANCH_W, tm), lambda i: (0, i)),
            pl.BlockSpec((N_HEADS, tm, LANES), lambda i: (0, i, 0)),
            pl.BlockSpec((tm, LANES), lambda i: (i, 0)),
            pl.BlockSpec((LANES, tm), lambda i: (0, i)),
        ],
        scratch_shapes=[pltpu.VMEM((8, LANES), F32)],
    )
    out_shape = [
        jax.ShapeDtypeStruct((4 * N_HEADS, m, LANES), BF16),
        jax.ShapeDtypeStruct((2 * BRANCH_W, m), BF16),
        jax.ShapeDtypeStruct((N_HEADS, m, LANES), BF16),
        jax.ShapeDtypeStruct((m, LANES), BF16),
        jax.ShapeDtypeStruct((LANES, m), BF16),
    ]
    return pl.pallas_call(
        functools.partial(_pre_kernel, tm=tm, seq=seq),
        grid_spec=grid_spec,
        out_shape=out_shape,
        compiler_params=pltpu.CompilerParams(
            dimension_semantics=("arbitrary",), vmem_limit_bytes=VMEM_LIMIT_BYTES),
        name="pre",
    )(x, lw["norm_g"], lw["w1"], lw["wuq"], lw["wuk"], lw["wuv"], consts["aug_e"], consts["tri"],
      lw["q_norm_g"], lw["kv_norm_g"], lw["b_f"], consts["rope_tab"])


def _swa_kernel(q_ref, k_ref, vt_ref, bias_ref, sink_ref, o_ref, ot_scr, *, seq):
    nb = seq // BLK
    half = N_HEADS // 2 * BLK
    sink = sink_ref[...]
    def scores(n):
        q = q_ref[:, n * BLK:(n + 1) * BLK, :].reshape(N_HEADS * BLK, LANES)
        return _dot_nt(k_ref[max(n - 1, 0) * BLK:(n + 1) * BLK, :], q)

    s_next = scores(0)
    for n in range(nb):
        k0 = max(n - 1, 0) * BLK
        k1 = (n + 1) * BLK
        if n == 0:
            bias = bias_ref[2, BLK:, :]
        else:
            bias = bias_ref[0 if n >= 2 else 1]
        s = jnp.minimum(s_next, -NEG) + bias
        if n + 1 < nb:
            s_next = scores(n + 1)
        m = jnp.maximum(jnp.max(s, axis=0, keepdims=True), sink)
        pb = jnp.exp2(s - m).astype(BF16)
        sink_p = jnp.exp2(sink - m)
        ones = jnp.ones((SUM_ROWS, k1 - k0), BF16)
        for kv in range(A_KV_HEADS):
            vt = jnp.concatenate([vt_ref[kv * HEAD_DIM:(kv + 1) * HEAD_DIM, k0:k1], ones], axis=0)
            pv = _dot(vt, pb[:, kv * half:(kv + 1) * half])
            pv = pv[:HEAD_DIM] / (pv[HEAD_DIM:HEAD_DIM + 1] + sink_p[:, kv * half:(kv + 1) * half])
            for g in range(N_HEADS // A_KV_HEADS):
                h = kv * (N_HEADS // A_KV_HEADS) + g
                ot_scr[h * HEAD_DIM:(h + 1) * HEAD_DIM, n * BLK:(n + 1) * BLK] = pv[:, g * BLK:(g + 1) * BLK]
    for c in range(nb):
        o_ref[c * BLK:(c + 1) * BLK, :] = ot_scr[:, c * BLK:(c + 1) * BLK].T.astype(o_ref.dtype)


def _swa_call(qa, ka, vta, bias, sink, l, *, batch, seq):
    m = ka.shape[0]
    grid_spec = pl.GridSpec(
        grid=(batch,),
        in_specs=[
            pl.BlockSpec((N_HEADS, seq, LANES), lambda b: (0, b, 0)),
            pl.BlockSpec((seq, LANES), lambda b: (b, 0)),
            pl.BlockSpec((LANES, seq), lambda b: (0, b)),
            _const_spec((3, 2 * BLK, N_HEADS * BLK)),
            _layer_spec((1, N_HEADS * BLK), l),
        ],
        out_specs=pl.BlockSpec((seq, BRANCH_W), lambda b: (b, 0)),
        scratch_shapes=[pltpu.VMEM((BRANCH_W, seq), F32)],
    )
    return pl.pallas_call(
        functools.partial(_swa_kernel, seq=seq),
        grid_spec=grid_spec,
        out_shape=jax.ShapeDtypeStruct((m, BRANCH_W), BF16),
        compiler_params=pltpu.CompilerParams(
            dimension_semantics=("parallel",), vmem_limit_bytes=VMEM_LIMIT_BYTES),
        name="swa",
    )(qa, ka, vta, bias, sink)


def _dense_kernel(q_ref, k_ref, vt_ref, dcap_ref, mqcap_ref, o_ref, ot_scr, *, seq):
    def head_chains(c):
        groups = [[(c, 0, BLK, [([(0, BLK)], [(0, BLK)], [(mqcap_ref, 0, BLK)], 0)])]]
        for u in range((seq - BLK) // Q_SUB):
            q0 = BLK + Q_SUB * u
            n_full = u * Q_SUB // K_TILE
            steps = [([(BLK + K_TILE * j, K_TILE)],) * 2 + (None, 0) for j in range(n_full)]
            rows = q0 + Q_SUB - (BLK + K_TILE * n_full)
            diag = (BLK + K_TILE * n_full, rows)
            steps.append(([(PAD, N_META), diag], [(0, BLK), diag],
                          [(dcap_ref, PAD, N_META), (dcap_ref, BLK + K_TILE - rows, rows)], PAD))
            if u % CHAINS_PER_GROUP == 0:
                groups.append([])
            groups[-1].append((c, q0, Q_SUB, steps))
        return groups

    def gather(ref_rows, ranges, axis):
        parts = [ref_rows(a, n) for a, n in ranges]
        return parts[0] if len(parts) == 1 else jnp.concatenate(parts, axis=axis)

    def head_body(hg, carry):
        heads = [hg * HEADS_PER_ITER + c for c in range(HEADS_PER_ITER)]
        rows = [pl.ds(pl.multiple_of(h * HEAD_DIM, HEAD_DIM), HEAD_DIM) for h in heads]
        per_head = [head_chains(c) for c in range(HEADS_PER_ITER)]
        slots = []
        for gi in range(len(per_head[0])):
            group = [ch for groups in per_head for ch in groups[gi]]
            for t in range(len(group[0][3])):
                slots += [(ch, t) for ch in group]

        def scores(slot):
            (c, q0, qn, steps), t = slot
            keys = gather(lambda a, n: k_ref[heads[c], a:a + n, :], steps[t][0], 0)
            return _dot_nt(keys, q_ref[heads[c], q0:q0 + qn, :])

        ahead = [scores(sl) for sl in slots[:LOOKAHEAD]]
        state = {}
        for n, ((c, q0, qn, steps), t) in enumerate(slots):
            s = ahead.pop(0)
            if n + LOOKAHEAD < len(slots):
                ahead.append(scores(slots[n + LOOKAHEAD]))
            _, value_ranges, cap_ranges, zero_rows = steps[t]
            if cap_ranges is not None:
                cap = [ref[a:a + n, :] for ref, a, n in cap_ranges]
                s = jnp.minimum(s, cap[0] if len(cap) == 1 else jnp.concatenate(cap, axis=0))
            mc = jnp.max(s, axis=0, keepdims=True)
            prev = state.get((c, q0))
            m_new = mc if prev is None else jnp.maximum(prev[0], mc)
            pt = jnp.exp2(s - m_new).astype(BF16)
            if zero_rows:
                pt = jnp.concatenate([jnp.zeros((zero_rows, qn), BF16), pt], axis=0)
            vt = gather(lambda a, n: vt_ref[rows[c], a:a + n], value_ranges, 1)
            vt = jnp.concatenate([vt, jnp.ones((SUM_ROWS, vt.shape[1]), BF16)], axis=0)
            pv = _dot(vt, pt)
            acc = pv if prev is None else jnp.exp2(prev[0] - m_new) * prev[1] + pv
            state[c, q0] = (m_new, acc)
            if t == len(steps) - 1:
                del state[c, q0]
                ot_scr[rows[c], q0:q0 + qn] = acc[:HEAD_DIM] / acc[HEAD_DIM:HEAD_DIM + 1]
        return carry

    lax.fori_loop(0, N_HEADS // HEADS_PER_ITER, head_body, 0)
    for c in range(seq // BLK):
        o_ref[0, c * BLK:(c + 1) * BLK, :] = ot_scr[:, c * BLK:(c + 1) * BLK].T.astype(o_ref.dtype)


def _dense_call(qk, vt, consts, *, batch, seq):
    m = qk.shape[1]
    grid_spec = pl.GridSpec(
        grid=(batch, 2),
        in_specs=[
            pl.BlockSpec((N_HEADS, seq, LANES), lambda b, x: (x, b, 0)),
            pl.BlockSpec((N_HEADS, seq, LANES), lambda b, x: (2 + x, b, 0)),
            pl.BlockSpec((BRANCH_W, seq), lambda b, x: (x, b)),
            _const_spec((BLK + K_TILE, Q_SUB)),
            _const_spec((BLK, BLK)),
        ],
        out_specs=pl.BlockSpec((1, seq, BRANCH_W), lambda b, x: (x, b, 0)),
        scratch_shapes=[pltpu.VMEM((BRANCH_W, seq), F32)],
    )
    return pl.pallas_call(
        functools.partial(_dense_kernel, seq=seq),
        grid_spec=grid_spec,
        out_shape=jax.ShapeDtypeStruct((2, m, BRANCH_W), BF16),
        compiler_params=pltpu.CompilerParams(
            dimension_semantics=("parallel", "parallel"), vmem_limit_bytes=VMEM_LIMIT_BYTES),
        name="dense",
    )(qk, qk, vt, consts["dcap"], consts["mqcap"])


def _post_kernel(x_ref, oa_ref, obc_ref, g_ref, wz_ref, wg_ref, wbr_ref, wout_ref, fg_ref, out_ref, *, final):
    x = x_ref[...]
    hb = _rms(x, g_ref[...]).astype(BF16)
    y = None
    for i in range(N_BRANCH):
        o = (oa_ref[...] if i == 0 else obc_ref[i - 1]).astype(F32)
        z = _dot(hb, wz_ref[:, BRANCH_W * i:BRANCH_W * (i + 1)])
        gate = jax.nn.sigmoid(_dot(hb, wg_ref[:, D_MODEL * i:D_MODEL * (i + 1)]))
        u = (o * (z * jax.nn.sigmoid(z))).astype(BF16)
        t = _dot(u, wbr_ref[i])
        y = gate * t if y is None else y + gate * t
    out = x + _dot(y.astype(BF16), wout_ref[...])
    if final:
        out = _rms(out, fg_ref[...])
    out_ref[...] = out


def _post_call(x, oa, obc, lw, l, final_g, *, final, batch, seq):
    m = x.shape[0]
    tm = POST_TM
    if final:
        n_t = (seq - BLK) // tm
        grid = (batch, n_t)
        m_out = batch * (seq - BLK)

        def row0(b, t):
            return pl.multiple_of(b * seq + BLK + t * tm, BLK)

        el = pl.Element
        stream_specs = [
            pl.BlockSpec((el(tm), el(D_MODEL)), lambda b, t: (row0(b, t), 0)),
            pl.BlockSpec((el(tm), el(BRANCH_W)), lambda b, t: (row0(b, t), 0)),
            pl.BlockSpec((el(2), el(tm), el(BRANCH_W)), lambda b, t: (0, row0(b, t), 0)),
        ]
        out_spec = pl.BlockSpec((tm, D_MODEL), lambda b, t: (b * n_t + t, 0))
    else:
        grid = (m // tm,)
        m_out = m
        stream_specs = [
            pl.BlockSpec((tm, D_MODEL), lambda i: (i, 0)),
            pl.BlockSpec((tm, BRANCH_W), lambda i: (i, 0)),
            pl.BlockSpec((2, tm, BRANCH_W), lambda i: (0, i, 0)),
        ]
        out_spec = pl.BlockSpec((tm, D_MODEL), lambda i: (i, 0))
    grid_spec = pl.GridSpec(
        grid=grid,
        in_specs=stream_specs + [
            _layer_spec((1, D_MODEL), l),
            _layer_spec((D_MODEL, N_BRANCH * BRANCH_W), l),
            _layer_spec((D_MODEL, N_BRANCH * D_MODEL), l),
            _layer_spec((N_BRANCH, BRANCH_W, D_MODEL), l),
            _layer_spec((D_MODEL, D_MODEL), l),
            _const_spec((1, D_MODEL)),
        ],
        out_specs=out_spec,
    )
    return pl.pallas_call(
        functools.partial(_post_kernel, final=final),
        grid_spec=grid_spec,
        out_shape=jax.ShapeDtypeStruct((m_out, D_MODEL), F32),
        input_output_aliases={} if final else {0: 0},
        compiler_params=pltpu.CompilerParams(
            dimension_semantics=("parallel",) * len(grid), vmem_limit_bytes=VMEM_LIMIT_BYTES),
        name="post",
    )(x, oa, obc, lw["norm_g"], lw["wz"], lw["wg"], lw["w_br"], lw["w_out"], final_g)


def _rope_tables(seq, tm):
    pos = (jnp.arange(seq) - PAD).astype(F32)
    inv = ROPE_THETA ** (-jnp.arange(ROPE_HALF, dtype=F32) / ROPE_HALF)
    ang = pos[:, None] * inv[None, :]
    cos, sin = jnp.cos(ang), jnp.sin(ang)
    zero = jnp.zeros_like(cos)

    def lanes(nope, a, b):
        return jnp.concatenate([jnp.full((seq, C_NOPE), nope, F32), a, b,
                                jnp.zeros((seq, LANES - C_NOPE - C_ROPE), F32)], axis=1)

    tabs = jnp.stack([lanes(1.0, cos, cos) * SCALE_C, lanes(0.0, -sin, zero) * SCALE_C,
                      lanes(0.0, zero, sin) * SCALE_C,
                      lanes(0.0, cos, cos), lanes(0.0, -sin, zero), lanes(0.0, zero, sin)])
    return jnp.tile(tabs, (1, tm // LANES, 1))


def _aug_matrix():
    e = np.zeros((LANES, LANES), np.float32)
    for h in range(N_HEADS):
        q0, k0 = AUG_W * h, AUG_K0 + AUG_W * h
        for p in range(3):
            e[p * N_HEADS + h, q0 + p] = 1.0
            e[AUG_ONE, q0 + 3 + p] = 1.0
            e[AUG_ONE, k0 + p] = 1.0
            e[p * N_HEADS + h, k0 + 3 + p] = -1.0
    return jnp.asarray(e, BF16)


def _dense_caps():
    meta_ok = (np.arange(BLK)[:, None] >= PAD) & np.ones((1, Q_SUB), bool)
    kk = np.arange(K_TILE)[:, None] - (K_TILE - Q_SUB)
    diag_ok = kk <= np.arange(Q_SUB)[None, :]
    dcap = np.where(np.concatenate([meta_ok, diag_ok]), -NEG, NEG)
    km = np.arange(BLK)[:, None]
    mqcap = np.where((km >= PAD) & (km <= np.arange(BLK)[None, :]), -NEG, NEG)
    return jnp.asarray(dcap, F32), jnp.asarray(mqcap, F32)


def _swa_biases():
    slopes = 2.0 ** (-8.0 * (jnp.arange(N_HEADS, dtype=F32) + 1.0) / N_HEADS)
    s = np.arange(2 * BLK)[:, None]
    t = np.arange(BLK)[None, :]
    dist = BLK + t - s
    window = (dist >= 0) & (dist < BLK)
    tabs = []
    for mask in (window, window & (s >= PAD), window & (s >= BLK + PAD)):
        per_head = [jnp.where(mask, -slopes[h] * dist.astype(np.float32) * LOG2E, NEG) for h in range(N_HEADS)]
        tabs.append(jnp.concatenate(per_head, axis=1))
    return jnp.stack(tabs).astype(F32)


def _stacked_weights(norm_g, w_in, b_f, sinks, q_norm_g, kv_norm_g, w_uq, w_ukv, w_br, w_out):
    depth = w_in.shape[0]
    splits = np.cumsum([0, 512, 128, 128, 512, 512, 512, 8, 256, 128, 32, 1536, 3072])
    aq, ak, av, bq, bk, bv, bfl, cq, ckv, ckr, z, g = [w_in[:, :, splits[i]:splits[i + 1]] for i in range(12)]
    aq = aq.reshape(depth, D_MODEL, A_KV_HEADS, N_HEADS // A_KV_HEADS, HEAD_DIM)
    aq = jnp.transpose(aq, (0, 1, 3, 2, 4)).reshape(depth, D_MODEL, N_HEADS * HEAD_DIM)
    misc = jnp.concatenate([bfl, jnp.zeros((depth, D_MODEL, MISC_ROPE - N_HEADS), F32), ckr,
                            jnp.zeros((depth, D_MODEL, LANES - MISC_ROPE - C_ROPE), F32)], axis=2)
    w1 = jnp.concatenate([aq, ak, av, bq, bk, bv, cq, ckv, misc], axis=2).astype(BF16)
    wuq = jnp.pad(w_uq.reshape(depth, C_Q_RANK, N_HEADS, C_NOPE + C_ROPE),
                  ((0, 0), (0, 0), (0, 0), (0, LANES - C_NOPE - C_ROPE))).reshape(depth, C_Q_RANK, N_HEADS * LANES)
    wukv = w_ukv.reshape(depth, C_KV_RANK, N_HEADS, 2 * HEAD_DIM)
    wuk = jnp.pad(wukv[..., :C_NOPE], ((0, 0), (0, 0), (0, 0), (0, LANES - C_NOPE)))
    wuk = wuk.reshape(depth, C_KV_RANK, N_HEADS * LANES)
    wuv = wukv[..., C_NOPE:].reshape(depth, C_KV_RANK, BRANCH_W)
    return {
        "norm_g": norm_g.reshape(depth, 1, D_MODEL), "w1": w1, "wuq": wuq.astype(BF16), "wuk": wuk.astype(BF16),
        "wuv": wuv.astype(BF16), "q_norm_g": q_norm_g.reshape(depth, 1, C_Q_RANK),
        "kv_norm_g": kv_norm_g.reshape(depth, 1, C_KV_RANK),
        "b_f": jnp.pad(b_f, ((0, 0), (0, LANES - N_HEADS))).reshape(depth, 1, LANES),
        "sink": jnp.repeat(sinks * LOG2E, BLK, axis=1).reshape(depth, 1, N_HEADS * BLK),
        "wz": z.astype(BF16), "wg": g.astype(BF16), "w_br": w_br.astype(BF16), "w_out": w_out.astype(BF16),
    }


def kernel(x, meta_tokens, norm_g, w_in, b_f, sinks, q_norm_g, kv_norm_g, w_uq, w_ukv, w_br, w_out, final_norm_g):
    batch, n_real, _ = x.shape
    seq = BLK + n_real
    depth = w_in.shape[0]
    assert (seq - BLK) % (Q_SUB * CHAINS_PER_GROUP) == 0 and K_TILE % Q_SUB == 0 and (batch * seq) % PRE_TM == 0 and (batch * seq) % POST_TM == 0
    h = jnp.concatenate([jnp.zeros((batch, PAD, D_MODEL), x.dtype),
                         jnp.broadcast_to(meta_tokens[None].astype(x.dtype), (batch, N_META, D_MODEL)),
                         x], axis=1).reshape(batch * seq, D_MODEL)
    dcap, mqcap = _dense_caps()
    consts = {"aug_e": _aug_matrix(), "tri": jnp.asarray(np.tri(PRE_TM), BF16),
              "rope_tab": _rope_tables(seq, PRE_TM), "dcap": dcap, "mqcap": mqcap}
    swa_bias = _swa_biases()
    final_g = final_norm_g.reshape(1, D_MODEL)
    lw = _stacked_weights(norm_g, w_in, b_f, sinks, q_norm_g, kv_norm_g, w_uq, w_ukv, w_br, w_out)
    for l in range(depth):
        qk, vt, qa, ka, vta = _pre_call(h, lw, l, consts, seq=seq)
        oa = _swa_call(qa, ka, vta, swa_bias, lw["sink"], l, batch=batch, seq=seq)
        obc = _dense_call(qk, vt, consts, batch=batch, seq=seq)
        h = _post_call(h, oa, obc, lw, l, final_g, final=(l == depth - 1), batch=batch, seq=seq)
    return h.reshape(batch, n_real, D_MODEL)
```

```python
import functools
import math

import jax
import jax.numpy as jnp
import numpy as np
from jax import lax
from jax.experimental import pallas as pl
from jax.experimental.pallas import tpu as pltpu

D_MODEL = 1024
N_META = 16
BLK = 128
PAD = BLK - N_META
HEAD_DIM = 64
N_HEADS = 8
A_KV_HEADS = 2
C_Q_RANK = 256
C_KV_RANK = 128
C_NOPE = 64
C_ROPE = 32
ROPE_HALF = C_ROPE // 2
ROPE_THETA = 10000.0
N_BRANCH = 3
BRANCH_W = 512
EPS = 1e-6
NEG = -1e30
LOG2E = math.log2(math.e)

LANES = 128
Q_SUB = 256
K_TILE = 512
CHAINS_PER_GROUP = 2
LOOKAHEAD = 3
HEADS_PER_ITER = 4
SUM_ROWS = 16
PRE_TM = 512
POST_TM = 1024
VMEM_LIMIT_BYTES = 56 * 1024 * 1024

SCALE_AB = HEAD_DIM ** -0.5 * LOG2E
SCALE_C = (C_NOPE + C_ROPE) ** -0.5 * LOG2E

COL_AQ = 0
COL_AK = 512
COL_AV = 640
COL_BQ = 768
COL_BK = 1280
COL_BV = 1792
COL_CQ = 2304
COL_CKV = 2560
COL_MISC = 2688
N_PRE = 2816
MISC_ROPE = 64
AUG_ONE = 24
AUG_W = 6
AUG_K0 = N_HEADS * AUG_W

BF16 = jnp.bfloat16
F32 = jnp.float32


def _dot(a, b):
    return jnp.dot(a, b, preferred_element_type=F32)


def _dot_nt(a, b):
    return lax.dot_general(a, b, (((1,), (1,)), ((), ())), preferred_element_type=F32)


def _rms(x, g):
    return x * lax.rsqrt(jnp.mean(x * x, axis=-1, keepdims=True) + EPS) * g


def _split3(x):
    hi = x.astype(BF16).astype(F32)
    r = x - hi
    mid = r.astype(BF16).astype(F32)
    lo = (r - mid).astype(BF16).astype(F32)
    return hi, mid, lo


def _pre_kernel(x_ref, g_ref, w1_ref, wuq_ref, wuk_ref, wuv_ref, e_ref, tri_ref, qng_ref, kvng_ref,
                bf_ref, tab_ref, qk_ref, vt_ref, qa_ref, ka_ref, vta_ref, carry_ref, *, tm, seq):
    i = pl.program_id(0)

    @pl.when(i == 0)
    def _():
        carry_ref[...] = jnp.zeros_like(carry_ref)

    hb = _rms(x_ref[...], g_ref[...]).astype(BF16)
    lane = lax.broadcasted_iota(jnp.int32, (tm, LANES), 1)
    row = lax.broadcasted_iota(jnp.int32, (tm, LANES), 0)
    lo_half = lane < HEAD_DIM

    misc = _dot(hb, w1_ref[:, COL_MISC:N_PRE])
    rc = _dot(hb, w1_ref[:, COL_CQ:COL_MISC])
    ra = _dot(hb, w1_ref[:, COL_AQ:COL_BQ])

    for j in range(N_HEADS // 2):
        pq = ra[:, LANES * j:LANES * (j + 1)] * SCALE_AB
        qa_ref[j] = jnp.where(lo_half, pq, 0.0).astype(BF16)
        qa_ref[j + N_HEADS // 2] = jnp.where(lo_half, 0.0, pq).astype(BF16)
    ka_ref[...] = ra[:, COL_AK:COL_AV].astype(BF16)
    vta_ref[...] = ra[:, COL_AV:COL_BQ].T.astype(BF16)

    p0 = lax.rem(i * tm, seq)
    pib = p0 + row
    pib = jnp.where(pib >= seq, pib - seq, pib)
    xg = misc + bf_ref[...]
    logf = (jnp.minimum(xg, 0.0) - jnp.log1p(jnp.exp(-jnp.abs(xg)))) * LOG2E
    logf = jnp.where((pib >= PAD) & (lane < N_HEADS), logf, 0.0)

    def pieces(v):
        rep = v + pltpu.roll(v, N_HEADS, 1) + pltpu.roll(v, 2 * N_HEADS, 1)
        hi, mid, lo = _split3(rep)
        return jnp.where(lane < N_HEADS, hi, jnp.where(lane < 2 * N_HEADS, mid, lo))

    def gather_pieces(v):
        return v + pltpu.roll(v, LANES - N_HEADS, 1) + pltpu.roll(v, LANES - 2 * N_HEADS, 1)

    bnd = seq - p0
    cum = gather_pieces(_dot(tri_ref[...], pieces(logf).astype(BF16)))
    before = jnp.sum(jnp.where(row < bnd, logf, 0.0), axis=0, keepdims=True)

    def rope(v, c, s1, s2):
        return v * c + pltpu.roll(v, LANES - ROPE_HALF, 1) * s1 + pltpu.roll(v, ROPE_HALF, 1) * s2

    cqn = _rms(rc[:, :C_Q_RANK], qng_ref[...]).astype(BF16)
    ckvn = _rms(rc[:, C_Q_RANK:], kvng_ref[...]).astype(BF16)
    qc = _dot(cqn, wuq_ref[...])
    for h in range(N_HEADS):
        blk = qc[:, LANES * h:LANES * (h + 1)]
        qk_ref[N_HEADS + h] = rope(blk, tab_ref[0], tab_ref[1], tab_ref[2]).astype(BF16)

    kc = _dot(ckvn, wuk_ref[...])
    kr = rope(misc, tab_ref[3], tab_ref[4], tab_ref[5])
    for h in range(N_HEADS):
        qk_ref[3 * N_HEADS + h] = jnp.where(lo_half, kc[:, LANES * h:LANES * (h + 1)], kr).astype(BF16)
    vt_ref[BRANCH_W:2 * BRANCH_W, :] = _dot(ckvn, wuv_ref[...]).T.astype(BF16)

    vt_ref[0:BRANCH_W, :] = _dot(hb, w1_ref[:, COL_BV:COL_CQ]).T.astype(BF16)

    carry_in = jnp.where(p0 == 0, 0.0, carry_ref[0:1, :])
    cum = jnp.where(lane < N_HEADS, cum + jnp.where(row >= bnd, -before, carry_in), 0.0)
    carry_ref[...] = jnp.broadcast_to(cum[tm - 1:tm, :], carry_ref.shape)
    pc = jnp.where(lane == AUG_ONE, 1.0, pieces(cum)).astype(BF16)
    aug = _dot(pc, e_ref[...])

    rb = _dot(hb, w1_ref[:, COL_BQ:COL_BV])
    for h in range(N_HEADS):
        j, t = divmod(h, 2)
        own = lo_half if t == 0 else jnp.logical_not(lo_half)
        a = HEAD_DIM if t == 0 else 0
        in_aug = (lane >= a) & (lane < a + AUG_W)
        pq = rb[:, LANES * j:LANES * (j + 1)] * SCALE_AB
        pk = rb[:, BRANCH_W + LANES * j:BRANCH_W + LANES * (j + 1)]
        aq = pltpu.roll(aug, (a - AUG_W * h) % LANES, 1)
        ak = pltpu.roll(aug, (a - AUG_K0 - AUG_W * h) % LANES, 1)
        qk_ref[h] = jnp.where(own, pq, jnp.where(in_aug, aq, 0.0)).astype(BF16)
        qk_ref[2 * N_HEADS + h] = jnp.where(own, pk, jnp.where(in_aug, ak, 0.0)).astype(BF16)


def _const_spec(shape):
    nd = len(shape)
    return pl.BlockSpec(shape, lambda *_: (0,) * nd, pipeline_mode=pl.Buffered(1))


def _layer_spec(shape, l):
    nd = len(shape)
    return pl.BlockSpec((None,) + shape, lambda *_: (l,) + (0,) * nd, pipeline_mode=pl.Buffered(1))


def _pre_call(x, lw, l, consts, *, seq):
    m = x.shape[0]
    tm = PRE_TM
    n_tiles = m // tm
    period = seq // LANES
    grid_spec = pl.GridSpec(
        grid=(n_tiles,),
        in_specs=[
            pl.BlockSpec((tm, D_MODEL), lambda i: (i, 0)),
            _layer_spec((1, D_MODEL), l),
            _layer_spec((D_MODEL, N_PRE), l),
            _layer_spec((C_Q_RANK, N_HEADS * LANES), l),
            _layer_spec((C_KV_RANK, N_HEADS * LANES), l),
            _layer_spec((C_KV_RANK, BRANCH_W), l),
            _const_spec((LANES, LANES)),
            _const_spec((tm, tm)),
            _layer_spec((1, C_Q_RANK), l),
            _layer_spec((1, C_KV_RANK), l),
            _layer_spec((1, LANES), l),
            pl.BlockSpec((6, tm, LANES), lambda i: (0, i % period, 0)),
        ],
        out_specs=[
            pl.BlockSpec((4 * N_HEADS, tm, LANES), lambda i: (0, i, 0)),
            pl.BlockSpec((2 * BRANCH_W, tm), lambda i: (0, i)),
            pl.BlockSpec((N_HEADS, tm, LANES), lambda i: (0, i, 0)),
            pl.BlockSpec((tm, LANES), lambda i: (i, 0)),
            pl.BlockSpec((LANES, tm), lambda i: (0, i)),
        ],
        scratch_shapes=[pltpu.VMEM((8, LANES), F32)],
    )
    out_shape = [
        jax.ShapeDtypeStruct((4 * N_HEADS, m, LANES), BF16),
        jax.ShapeDtypeStruct((2 * BRANCH_W, m), BF16),
        jax.ShapeDtypeStruct((N_HEADS, m, LANES), BF16),
        jax.ShapeDtypeStruct((m, LANES), BF16),
        jax.ShapeDtypeStruct((LANES, m), BF16),
    ]
    return pl.pallas_call(
        functools.partial(_pre_kernel, tm=tm, seq=seq),
        grid_spec=grid_spec,
        out_shape=out_shape,
        compiler_params=pltpu.CompilerParams(
            dimension_semantics=("arbitrary",), vmem_limit_bytes=VMEM_LIMIT_BYTES),
        name="pre",
    )(x, lw["norm_g"], lw["w1"], lw["wuq"], lw["wuk"], lw["wuv"], consts["aug_e"], consts["tri"],
      lw["q_norm_g"], lw["kv_norm_g"], lw["b_f"], consts["rope_tab"])


def _swa_kernel(q_ref, k_ref, vt_ref, bias_ref, sink_ref, o_ref, *, seq):
    nb = seq // BLK
    half = N_HEADS // 2 * BLK
    sink = sink_ref[...]
    def scores(n):
        q = q_ref[:, n * BLK:(n + 1) * BLK, :].reshape(N_HEADS * BLK, LANES)
        return _dot_nt(k_ref[max(n - 1, 0) * BLK:(n + 1) * BLK, :], q)

    s_next = scores(0)
    for n in range(nb):
        k0 = max(n - 1, 0) * BLK
        k1 = (n + 1) * BLK
        if n == 0:
            bias = bias_ref[2, BLK:, :]
        else:
            bias = bias_ref[0 if n >= 2 else 1]
        s = jnp.minimum(s_next, -NEG) + bias
        if n + 1 < nb:
            s_next = scores(n + 1)
        m = jnp.maximum(jnp.max(s, axis=0, keepdims=True), sink)
        pb = jnp.exp2(s - m).astype(BF16)
        sink_p = jnp.exp2(sink - m)
        ones = jnp.ones((SUM_ROWS, k1 - k0), BF16)
        for kv in range(A_KV_HEADS):
            vt = jnp.concatenate([vt_ref[kv * HEAD_DIM:(kv + 1) * HEAD_DIM, k0:k1], ones], axis=0)
            pv = _dot(vt, pb[:, kv * half:(kv + 1) * half])
            pv = pv[:HEAD_DIM] / (pv[HEAD_DIM:HEAD_DIM + 1] + sink_p[:, kv * half:(kv + 1) * half])
            for g in range(N_HEADS // A_KV_HEADS):
                h = kv * (N_HEADS // A_KV_HEADS) + g
                o_ref[h * HEAD_DIM:(h + 1) * HEAD_DIM, n * BLK:(n + 1) * BLK] = (
                    pv[:, g * BLK:(g + 1) * BLK].astype(o_ref.dtype))


def _swa_call(qa, ka, vta, bias, sink, l, *, batch, seq):
    m = ka.shape[0]
    grid_spec = pl.GridSpec(
        grid=(batch,),
        in_specs=[
            pl.BlockSpec((N_HEADS, seq, LANES), lambda b: (0, b, 0)),
            pl.BlockSpec((seq, LANES), lambda b: (b, 0)),
            pl.BlockSpec((LANES, seq), lambda b: (0, b)),
            _const_spec((3, 2 * BLK, N_HEADS * BLK)),
            _layer_spec((1, N_HEADS * BLK), l),
        ],
        out_specs=pl.BlockSpec((BRANCH_W, seq), lambda b: (0, b)),
    )
    return pl.pallas_call(
        functools.partial(_swa_kernel, seq=seq),
        grid_spec=grid_spec,
        out_shape=jax.ShapeDtypeStruct((BRANCH_W, m), BF16),
        compiler_params=pltpu.CompilerParams(
            dimension_semantics=("parallel",), vmem_limit_bytes=VMEM_LIMIT_BYTES),
        name="swa",
    )(qa, ka, vta, bias, sink)


def _dense_kernel(q_ref, k_ref, vt_ref, dcap_ref, mqcap_ref, o_ref, *, seq):
    def head_chains(c):
        groups = [[(c, 0, BLK, [([(0, BLK)], [(0, BLK)], [(mqcap_ref, 0, BLK)], 0)])]]
        for u in range((seq - BLK) // Q_SUB):
            q0 = BLK + Q_SUB * u
            n_full = u * Q_SUB // K_TILE
            steps = [([(BLK + K_TILE * j, K_TILE)],) * 2 + (None, 0) for j in range(n_full)]
            rows = q0 + Q_SUB - (BLK + K_TILE * n_full)
            diag = (BLK + K_TILE * n_full, rows)
            steps.append(([(PAD, N_META), diag], [(0, BLK), diag],
                          [(dcap_ref, PAD, N_META), (dcap_ref, BLK + K_TILE - rows, rows)], PAD))
            if u % CHAINS_PER_GROUP == 0:
                groups.append([])
            groups[-1].append((c, q0, Q_SUB, steps))
        return groups

    def gather(ref_rows, ranges, axis):
        parts = [ref_rows(a, n) for a, n in ranges]
        return parts[0] if len(parts) == 1 else jnp.concatenate(parts, axis=axis)

    def head_body(hg, carry):
        heads = [hg * HEADS_PER_ITER + c for c in range(HEADS_PER_ITER)]
        rows = [pl.ds(pl.multiple_of(h * HEAD_DIM, HEAD_DIM), HEAD_DIM) for h in heads]
        per_head = [head_chains(c) for c in range(HEADS_PER_ITER)]
        slots = []
        for gi in range(len(per_head[0])):
            group = [ch for groups in per_head for ch in groups[gi]]
            for t in range(len(group[0][3])):
                slots += [(ch, t) for ch in group]

        def scores(slot):
            (c, q0, qn, steps), t = slot
            keys = gather(lambda a, n: k_ref[heads[c], a:a + n, :], steps[t][0], 0)
            return _dot_nt(keys, q_ref[heads[c], q0:q0 + qn, :])

        ahead = [scores(sl) for sl in slots[:LOOKAHEAD]]
        state = {}
        for n, ((c, q0, qn, steps), t) in enumerate(slots):
            s = ahead.pop(0)
            if n + LOOKAHEAD < len(slots):
                ahead.append(scores(slots[n + LOOKAHEAD]))
            _, value_ranges, cap_ranges, zero_rows = steps[t]
            if cap_ranges is not None:
                cap = [ref[a:a + n, :] for ref, a, n in cap_ranges]
                s = jnp.minimum(s, cap[0] if len(cap) == 1 else jnp.concatenate(cap, axis=0))
            mc = jnp.max(s, axis=0, keepdims=True)
            prev = state.get((c, q0))
            m_new = mc if prev is None else jnp.maximum(prev[0], mc)
            pt = jnp.exp2(s - m_new).astype(BF16)
            if zero_rows:
                pt = jnp.concatenate([jnp.zeros((zero_rows, qn), BF16), pt], axis=0)
            vt = gather(lambda a, n: vt_ref[rows[c], a:a + n], value_ranges, 1)
            vt = jnp.concatenate([vt, jnp.ones((SUM_ROWS, vt.shape[1]), BF16)], axis=0)
            pv = _dot(vt, pt)
            acc = pv if prev is None else jnp.exp2(prev[0] - m_new) * prev[1] + pv
            state[c, q0] = (m_new, acc)
            if t == len(steps) - 1:
                del state[c, q0]
                o_ref[0, rows[c], q0:q0 + qn] = (acc[:HEAD_DIM] / acc[HEAD_DIM:HEAD_DIM + 1]).astype(o_ref.dtype)
        return carry

    lax.fori_loop(0, N_HEADS // HEADS_PER_ITER, head_body, 0)


def _dense_call(qk, vt, consts, *, batch, seq):
    m = qk.shape[1]
    grid_spec = pl.GridSpec(
        grid=(batch, 2),
        in_specs=[
            pl.BlockSpec((N_HEADS, seq, LANES), lambda b, x: (x, b, 0)),
            pl.BlockSpec((N_HEADS, seq, LANES), lambda b, x: (2 + x, b, 0)),
            pl.BlockSpec((BRANCH_W, seq), lambda b, x: (x, b)),
            _const_spec((BLK + K_TILE, Q_SUB)),
            _const_spec((BLK, BLK)),
        ],
        out_specs=pl.BlockSpec((1, BRANCH_W, seq), lambda b, x: (x, 0, b)),
    )
    return pl.pallas_call(
        functools.partial(_dense_kernel, seq=seq),
        grid_spec=grid_spec,
        out_shape=jax.ShapeDtypeStruct((2, BRANCH_W, m), BF16),
        compiler_params=pltpu.CompilerParams(
            dimension_semantics=("parallel", "parallel"), vmem_limit_bytes=VMEM_LIMIT_BYTES),
        name="dense",
    )(qk, qk, vt, consts["dcap"], consts["mqcap"])


def _post_kernel(x_ref, oa_ref, obc_ref, g_ref, wz_ref, wg_ref, wbr_ref, wout_ref, fg_ref, out_ref, *, final):
    x = x_ref[...]
    hb = _rms(x, g_ref[...]).astype(BF16)
    y = None
    for i in range(N_BRANCH):
        o = (oa_ref[...] if i == 0 else obc_ref[i - 1]).astype(F32).T
        z = _dot(hb, wz_ref[:, BRANCH_W * i:BRANCH_W * (i + 1)])
        gate = jax.nn.sigmoid(_dot(hb, wg_ref[:, D_MODEL * i:D_MODEL * (i + 1)]))
        u = (o * (z * jax.nn.sigmoid(z))).astype(BF16)
        t = _dot(u, wbr_ref[i])
        y = gate * t if y is None else y + gate * t
    out = x + _dot(y.astype(BF16), wout_ref[...])
    if final:
        out = _rms(out, fg_ref[...])
    out_ref[...] = out


def _post_call(x, oa, obc, lw, l, final_g, *, final, batch, seq):
    m = x.shape[0]
    tm = POST_TM
    if final:
        n_t = (seq - BLK) // tm
        grid = (batch, n_t)
        m_out = batch * (seq - BLK)

        def row0(b, t):
            return pl.multiple_of(b * seq + BLK + t * tm, BLK)

        el = pl.Element
        stream_specs = [
            pl.BlockSpec((el(tm), el(D_MODEL)), lambda b, t: (row0(b, t), 0)),
            pl.BlockSpec((el(BRANCH_W), el(tm)), lambda b, t: (0, row0(b, t))),
            pl.BlockSpec((el(2), el(BRANCH_W), el(tm)), lambda b, t: (0, 0, row0(b, t))),
        ]
        out_spec = pl.BlockSpec((tm, D_MODEL), lambda b, t: (b * n_t + t, 0))
    else:
        grid = (m // tm,)
        m_out = m
        stream_specs = [
            pl.BlockSpec((tm, D_MODEL), lambda i: (i, 0)),
            pl.BlockSpec((BRANCH_W, tm), lambda i: (0, i)),
            pl.BlockSpec((2, BRANCH_W, tm), lambda i: (0, 0, i)),
        ]
        out_spec = pl.BlockSpec((tm, D_MODEL), lambda i: (i, 0))
    grid_spec = pl.GridSpec(
        grid=grid,
        in_specs=stream_specs + [
            _layer_spec((1, D_MODEL), l),
            _layer_spec((D_MODEL, N_BRANCH * BRANCH_W), l),
            _layer_spec((D_MODEL, N_BRANCH * D_MODEL), l),
            _layer_spec((N_BRANCH, BRANCH_W, D_MODEL), l),
            _layer_spec((D_MODEL, D_MODEL), l),
            _const_spec((1, D_MODEL)),
        ],
        out_specs=out_spec,
    )
    return pl.pallas_call(
        functools.partial(_post_kernel, final=final),
        grid_spec=grid_spec,
        out_shape=jax.ShapeDtypeStruct((m_out, D_MODEL), F32),
        input_output_aliases={} if final else {0: 0},
        compiler_params=pltpu.CompilerParams(
            dimension_semantics=("parallel",) * len(grid), vmem_limit_bytes=VMEM_LIMIT_BYTES),
        name="post",
    )(x, oa, obc, lw["norm_g"], lw["wz"], lw["wg"], lw["w_br"], lw["w_out"], final_g)


def _rope_tables(seq, tm):
    pos = (jnp.arange(seq) - PAD).astype(F32)
    inv = ROPE_THETA ** (-jnp.arange(ROPE_HALF, dtype=F32) / ROPE_HALF)
    ang = pos[:, None] * inv[None, :]
    cos, sin = jnp.cos(ang), jnp.sin(ang)
    zero = jnp.zeros_like(cos)

    def lanes(nope, a, b):
        return jnp.concatenate([jnp.full((seq, C_NOPE), nope, F32), a, b,
                                jnp.zeros((seq, LANES - C_NOPE - C_ROPE), F32)], axis=1)

    tabs = jnp.stack([lanes(1.0, cos, cos) * SCALE_C, lanes(0.0, -sin, zero) * SCALE_C,
                      lanes(0.0, zero, sin) * SCALE_C,
                      lanes(0.0, cos, cos), lanes(0.0, -sin, zero), lanes(0.0, zero, sin)])
    return jnp.tile(tabs, (1, tm // LANES, 1))


def _aug_matrix():
    e = np.zeros((LANES, LANES), np.float32)
    for h in range(N_HEADS):
        q0, k0 = AUG_W * h, AUG_K0 + AUG_W * h
        for p in range(3):
            e[p * N_HEADS + h, q0 + p] = 1.0
            e[AUG_ONE, q0 + 3 + p] = 1.0
            e[AUG_ONE, k0 + p] = 1.0
            e[p * N_HEADS + h, k0 + 3 + p] = -1.0
    return jnp.asarray(e, BF16)


def _dense_caps():
    meta_ok = (np.arange(BLK)[:, None] >= PAD) & np.ones((1, Q_SUB), bool)
    kk = np.arange(K_TILE)[:, None] - (K_TILE - Q_SUB)
    diag_ok = kk <= np.arange(Q_SUB)[None, :]
    dcap = np.where(np.concatenate([meta_ok, diag_ok]), -NEG, NEG)
    km = np.arange(BLK)[:, None]
    mqcap = np.where((km >= PAD) & (km <= np.arange(BLK)[None, :]), -NEG, NEG)
    return jnp.asarray(dcap, F32), jnp.asarray(mqcap, F32)


def _swa_biases():
    slopes = 2.0 ** (-8.0 * (jnp.arange(N_HEADS, dtype=F32) + 1.0) / N_HEADS)
    s = np.arange(2 * BLK)[:, None]
    t = np.arange(BLK)[None, :]
    dist = BLK + t - s
    window = (dist >= 0) & (dist < BLK)
    tabs = []
    for mask in (window, window & (s >= PAD), window & (s >= BLK + PAD)):
        per_head = [jnp.where(mask, -slopes[h] * dist.astype(np.float32) * LOG2E, NEG) for h in range(N_HEADS)]
        tabs.append(jnp.concatenate(per_head, axis=1))
    return jnp.stack(tabs).astype(F32)


def _stacked_weights(norm_g, w_in, b_f, sinks, q_norm_g, kv_norm_g, w_uq, w_ukv, w_br, w_out):
    depth = w_in.shape[0]
    splits = np.cumsum([0, 512, 128, 128, 512, 512, 512, 8, 256, 128, 32, 1536, 3072])
    aq, ak, av, bq, bk, bv, bfl, cq, ckv, ckr, z, g = [w_in[:, :, splits[i]:splits[i + 1]] for i in range(12)]
    aq = aq.reshape(depth, D_MODEL, A_KV_HEADS, N_HEADS // A_KV_HEADS, HEAD_DIM)
    aq = jnp.transpose(aq, (0, 1, 3, 2, 4)).reshape(depth, D_MODEL, N_HEADS * HEAD_DIM)
    misc = jnp.concatenate([bfl, jnp.zeros((depth, D_MODEL, MISC_ROPE - N_HEADS), F32), ckr,
                            jnp.zeros((depth, D_MODEL, LANES - MISC_ROPE - C_ROPE), F32)], axis=2)
    w1 = jnp.concatenate([aq, ak, av, bq, bk, bv, cq, ckv, misc], axis=2).astype(BF16)
    wuq = jnp.pad(w_uq.reshape(depth, C_Q_RANK, N_HEADS, C_NOPE + C_ROPE),
                  ((0, 0), (0, 0), (0, 0), (0, LANES - C_NOPE - C_ROPE))).reshape(depth, C_Q_RANK, N_HEADS * LANES)
    wukv = w_ukv.reshape(depth, C_KV_RANK, N_HEADS, 2 * HEAD_DIM)
    wuk = jnp.pad(wukv[..., :C_NOPE], ((0, 0), (0, 0), (0, 0), (0, LANES - C_NOPE)))
    wuk = wuk.reshape(depth, C_KV_RANK, N_HEADS * LANES)
    wuv = wukv[..., C_NOPE:].reshape(depth, C_KV_RANK, BRANCH_W)
    return {
        "norm_g": norm_g.reshape(depth, 1, D_MODEL), "w1": w1, "wuq": wuq.astype(BF16), "wuk": wuk.astype(BF16),
        "wuv": wuv.astype(BF16), "q_norm_g": q_norm_g.reshape(depth, 1, C_Q_RANK),
        "kv_norm_g": kv_norm_g.reshape(depth, 1, C_KV_RANK),
        "b_f": jnp.pad(b_f, ((0, 0), (0, LANES - N_HEADS))).reshape(depth, 1, LANES),
        "sink": jnp.repeat(sinks * LOG2E, BLK, axis=1).reshape(depth, 1, N_HEADS * BLK),
        "wz": z.astype(BF16), "wg": g.astype(BF16), "w_br": w_br.astype(BF16), "w_out": w_out.astype(BF16),
    }


def kernel(x, meta_tokens, norm_g, w_in, b_f, sinks, q_norm_g, kv_norm_g, w_uq, w_ukv, w_br, w_out, final_norm_g):
    batch, n_real, _ = x.shape
    seq = BLK + n_real
    depth = w_in.shape[0]
    assert (seq - BLK) % (Q_SUB * CHAINS_PER_GROUP) == 0 and K_TILE % Q_SUB == 0 and (batch * seq) % PRE_TM == 0 and (batch * seq) % POST_TM == 0
    h = jnp.concatenate([jnp.zeros((batch, PAD, D_MODEL), x.dtype),
                         jnp.broadcast_to(meta_tokens[None].astype(x.dtype), (batch, N_META, D_MODEL)),
                         x], axis=1).reshape(batch * seq, D_MODEL)
    dcap, mqcap = _dense_caps()
    consts = {"aug_e": _aug_matrix(), "tri": jnp.asarray(np.tri(PRE_TM), BF16),
              "rope_tab": _rope_tables(seq, PRE_TM), "dcap": dcap, "mqcap": mqcap}
    swa_bias = _swa_biases()
    final_g = final_norm_g.reshape(1, D_MODEL)
    lw = _stacked_weights(norm_g, w_in, b_f, sinks, q_norm_g, kv_norm_g, w_uq, w_ukv, w_br, w_out)
    for l in range(depth):
        qk, vt, qa, ka, vta = _pre_call(h, lw, l, consts, seq=seq)
        oa = _swa_call(qa, ka, vta, swa_bias, lw["sink"], l, batch=batch, seq=seq)
        obc = _dense_call(qk, vt, consts, batch=batch, seq=seq)
        h = _post_call(h, oa, obc, lw, l, final_g, final=(l == depth - 1), batch=batch, seq=seq)
    return h.reshape(batch, n_real, D_MODEL)
```
